```python
import math
import jax, jax.numpy as jnp
from jax import lax
import numpy as np

D_MODEL = 2048
BATCH = 4
SEQ = 4096
DEPTH = 4

EPS = 1e-5
POOL_WINDOWS = (2, 4, 8, 16)
POOL_GROUPS = 4
POOL_GROUP_DIM = D_MODEL // 16
POOL_DIM = POOL_GROUPS * POOL_GROUP_DIM
CONV_DIM = D_MODEL // 4
CONV_KERNEL = 31
ATTN_HEAD_DIM = 128
ATTN_HEADS = D_MODEL // 256
ATTN_DIM = ATTN_HEADS * ATTN_HEAD_DIM
Q_BLOCK = 128
GMLP_GROUPS = 4
GMLP_GROUP_DIM = D_MODEL // 16
GMLP_DIM = GMLP_GROUPS * GMLP_GROUP_DIM
GMLP_CHUNK = 128
IN_SIZES = (POOL_DIM, 2 * CONV_DIM, ATTN_DIM, ATTN_DIM, ATTN_DIM, ATTN_HEADS, 2 * GMLP_DIM)
IN_DIM = sum(IN_SIZES)
BRANCH_DIMS = (POOL_DIM, CONV_DIM, ATTN_DIM, GMLP_DIM)
N_BRANCHES = 4
MIX_DIM = sum(BRANCH_DIMS)
N_EXPERTS = 32
TOP_K = 4
D_FF_EXPERT = 3 * D_MODEL // 8
SWIGLU_LIMIT = 7.0
SWIGLU_ALPHA = 1.702
MOE_BLOCK = 128

kernel_name = "hybrid_pool_conv_fox_gmlp_moe_adaln"


def _split_points(sizes):
    pts, acc = [], 0
    for s in sizes[:-1]:
        acc += s
        pts.append(acc)
    return pts


def rms_norm(x, g):
    xf = x.astype(jnp.float32)
    y = xf * lax.rsqrt(jnp.mean(xf * xf, axis=-1, keepdims=True) + EPS)
    return (y * g.astype(jnp.float32)).astype(x.dtype)


def layer_norm(x, g, b):
    xf = x.astype(jnp.float32)
    mu = jnp.mean(xf, axis=-1, keepdims=True)
    xc = xf - mu
    y = xc * lax.rsqrt(jnp.mean(xc * xc, axis=-1, keepdims=True) + EPS)
    return (y * g.astype(jnp.float32) + b.astype(jnp.float32)).astype(x.dtype)


def modulate(h, shift, scale):
    return h * (1 + scale[:, None, :]) + shift[:, None, :]


def pool_mixer(a, pool_w, pool_scale):
    B, S, _ = a.shape
    af = a.reshape(B, S, POOL_GROUPS, POOL_GROUP_DIM).astype(jnp.float32)
    cs = jnp.pad(jnp.cumsum(af, axis=1), ((0, 0), (1, 0), (0, 0), (0, 0)))
    t = jnp.arange(S)
    pooled = []
    for g, w in enumerate(POOL_WINDOWS):
        csg = cs[:, :, g]
        upper = csg[:, 1:]
        lower = jnp.pad(csg[:, :S + 1 - w], ((0, 0), (w - 1, 0), (0, 0)))
        count = jnp.minimum(t + 1, w).astype(jnp.float32)[None, :, None]
        pooled.append((upper - lower) / count)
    mixed = (jnp.stack(pooled, axis=2) - af).astype(a.dtype)
    y = jnp.einsum('bsgc,gcd->bsgd', mixed, pool_w)
    return y.reshape(B, S, POOL_DIM) * pool_scale


def conv_mixer(z, conv_w, conv_b, ln_g, ln_b):
    za, zb = jnp.split(z, 2, axis=-1)
    glu = za * jax.nn.sigmoid(zb)
    y = lax.conv_general_dilated(glu, conv_w[:, None, :], window_strides=(1,),
                                 padding=[(CONV_KERNEL - 1, 0)],
                                 dimension_numbers=('NWC', 'WIO', 'NWC'),
                                 feature_group_count=CONV_DIM) + conv_b
    return jax.nn.silu(layer_norm(y, ln_g, ln_b))


def forgetting_attention(q, k, v, f_logit):
    B, S, H, Dh = q.shape
    F = jnp.cumsum(jax.nn.log_sigmoid(f_logit.astype(jnp.float32)), axis=1)
    nb = S // Q_BLOCK
    qb = q.reshape(B, nb, Q_BLOCK, H, Dh).transpose(1, 0, 2, 3, 4)
    Fq = F.reshape(B, nb, Q_BLOCK, H).transpose(1, 0, 2, 3)
    Fk = F.transpose(0, 2, 1)
    kpos = jnp.arange(S)
    scale = 1.0 / math.sqrt(Dh)

    def block(args):
        i, qi, Fqi = args
        s = jnp.einsum('bqhd,bkhd->bhqk', qi, k).astype(jnp.float32) * scale
        s = s + Fqi.transpose(0, 2, 1)[..., None] - Fk[:, :, None, :]
        qpos = i * Q_BLOCK + jnp.arange(Q_BLOCK)
        s = jnp.where(kpos[None, :] <= qpos[:, None], s, -jnp.inf)
        p = jax.nn.softmax(s, axis=-1)
        return jnp.einsum('bhqk,bkhd->bqhd', p.astype(v.dtype), v)

    out = lax.map(block, (jnp.arange(nb), qb, Fq))
    return out.transpose(1, 0, 2, 3, 4).reshape(B, S, H * Dh)


def gmlp_mixer(z, ln_g, ws, bs):
    z = jax.nn.gelu(z)
    u, v = jnp.split(z, 2, axis=-1)
    v = rms_norm(v, ln_g)
    B, S, _ = v.shape
    n = S // GMLP_CHUNK
    vc = v.reshape(B, n, GMLP_CHUNK, GMLP_GROUPS, GMLP_GROUP_DIM)
    causal = jnp.tril(jnp.ones((GMLP_CHUNK, GMLP_CHUNK), dtype=bool))
    wm = jnp.where(causal[None], ws, 0).astype(v.dtype)
    sv = jnp.einsum('gts,bnsgc->bntgc', wm, vc) + bs.T[None, None, :, :, None]
    return u * sv.reshape(B, S, GMLP_DIM)


def moe_ffn(h, router_w, router_b, w_up, b_up, w_down, b_down):
    B, S, D = h.shape
    T = B * S
    hf = h.reshape(T, D)
    logits = (hf @ router_w).astype(jnp.float32) + router_b.astype(jnp.float32)
    top_val, top_idx = lax.top_k(logits, TOP_K)
    gates = jax.nn.softmax(top_val, axis=-1)
    A = T * TOP_K
    flat_e = top_idx.reshape(A)
    flat_tok = jnp.arange(A, dtype=jnp.int32) // TOP_K
    flat_gate = gates.reshape(A)
    order = jnp.argsort(flat_e)
    sorted_e = flat_e[order]
    counts = jnp.bincount(flat_e, length=N_EXPERTS)
    padded = (counts + MOE_BLOCK - 1) // MOE_BLOCK * MOE_BLOCK
    start = jnp.cumsum(counts) - counts
    pend = jnp.cumsum(padded)
    pstart = pend - padded
    dest = pstart[sorted_e] + jnp.arange(A) - start[sorted_e]
    n_blocks = -(-(A + N_EXPERTS * (MOE_BLOCK - 1)) // MOE_BLOCK)
    NP = n_blocks * MOE_BLOCK
    tok_buf = jnp.full((NP,), T, dtype=jnp.int32).at[dest].set(flat_tok[order])
    gate_buf = jnp.zeros((NP,), jnp.float32).at[dest].set(flat_gate[order])
    block_e = jnp.minimum(jnp.searchsorted(pend, jnp.arange(n_blocks) * MOE_BLOCK, side='right'),
                          N_EXPERTS - 1)
    h_pad = jnp.concatenate([hf, jnp.zeros((1, D), hf.dtype)], axis=0)

    def expert_block(args):
        toks, e = args
        xb = h_pad[toks]
        gu = xb @ w_up[e] + b_up[e]
        g, lin = jnp.split(gu, 2, axis=-1)
        g = jnp.minimum(g, SWIGLU_LIMIT)
        lin = jnp.clip(lin, -SWIGLU_LIMIT, SWIGLU_LIMIT)
        act = g * jax.nn.sigmoid(SWIGLU_ALPHA * g) * (lin + 1)
        return act @ w_down[e] + b_down[e]

    out = lax.map(expert_block, (tok_buf.reshape(n_blocks, MOE_BLOCK), block_e))
    out = out.reshape(NP, D) * gate_buf[:, None].astype(out.dtype)
    y = jnp.zeros((T + 1, D), out.dtype).at[tok_buf].add(out)[:T]
    return y.reshape(B, S, D)


def setup_inputs(seed: int = 0) -> dict:
    key = jax.random.key(seed)
    ks = jax.random.split(key, 32)
    L, D, E, F = DEPTH, D_MODEL, N_EXPERTS, D_FF_EXPERT

    def nrm(k, shape, s):
        return jax.random.normal(k, shape, jnp.float32) * s

    return {
        'x': nrm(ks[0], (BATCH, SEQ, D), 1.0),
        'c': nrm(ks[1], (BATCH, D), 1.0),
        'w_ada': nrm(ks[2], (L, D, 6 * D), 0.2 * D ** -0.5),
        'b_ada': nrm(ks[3], (L, 6 * D), 0.02),
        'g_norm_mix': 1.0 + nrm(ks[4], (L, D), 0.02),
        'w_in': nrm(ks[5], (L, D, IN_DIM), D ** -0.5),
        'pool_w': nrm(ks[6], (L, POOL_GROUPS, POOL_GROUP_DIM, POOL_GROUP_DIM), POOL_GROUP_DIM ** -0.5),
        'pool_scale': 1.0 + nrm(ks[7], (L, POOL_DIM), 0.02),
        'conv_w': nrm(ks[8], (L, CONV_KERNEL, CONV_DIM), CONV_KERNEL ** -0.5),
        'conv_b': nrm(ks[9], (L, CONV_DIM), 0.02),
        'conv_ln_g': 1.0 + nrm(ks[10], (L, CONV_DIM), 0.02),
        'conv_ln_b': nrm(ks[11], (L, CONV_DIM), 0.02),
        'fgate_b': jax.random.uniform(ks[12], (L, ATTN_HEADS), jnp.float32, 1.0, 6.0),
        'gmlp_ln_g': 1.0 + nrm(ks[13], (L, GMLP_DIM), 0.02),
        'gmlp_ws': nrm(ks[14], (L, GMLP_GROUPS, GMLP_CHUNK, GMLP_CHUNK), 0.05),
        'gmlp_bs': 1.0 + nrm(ks[15], (L, GMLP_GROUPS, GMLP_CHUNK), 0.02),
        'w_gate': nrm(ks[16], (L, D, N_BRANCHES * D), D ** -0.5),
        'b_gate': nrm(ks[17], (L, N_BRANCHES * D), 0.02),
        'w_branch': nrm(ks[18], (L, MIX_DIM, D), (MIX_DIM / N_BRANCHES) ** -0.5),
        'w_o': nrm(ks[19], (L, D, D), D ** -0.5),
        'g_norm_ffn': 1.0 + nrm(ks[20], (L, D), 0.02),
        'router_w': nrm(ks[21], (L, D, E), D ** -0.5),
        'router_b': nrm(ks[22], (L, E), 0.01),
        'w_up': nrm(ks[23], (L, E, D, 2 * F), D ** -0.5),
        'b_up': nrm(ks[24], (L, E, 2 * F), 0.02),
        'w_down': nrm(ks[25], (L, E, F, D), F ** -0.5),
        'b_down': nrm(ks[26], (L, E, D), 0.02),
        'g_final': 1.0 + nrm(ks[27], (D,), 0.02),
    }


def reference(x, c, w_ada, b_ada, g_norm_mix, w_in, pool_w, pool_scale, conv_w, conv_b, conv_ln_g,
              conv_ln_b, fgate_b, gmlp_ln_g, gmlp_ws, gmlp_bs, w_gate, b_gate, w_branch, w_o,
              g_norm_ffn, router_w, router_b, w_up, b_up, w_down, b_down, g_final):
    B, S, D = x.shape
    in_pts = _split_points(IN_SIZES)
    br_pts = [0] + _split_points(BRANCH_DIMS) + [MIX_DIM]
    c_act = jax.nn.silu(c)
    for l in range(DEPTH):
        mod = c_act @ w_ada[l] + b_ada[l]
        sh1, sc1, gt1, sh2, sc2, gt2 = jnp.split(mod, 6, axis=-1)
        h = modulate(rms_norm(x, g_norm_mix[l]), sh1, sc1)
        z = h @ w_in[l]
        z_pool, z_conv, q, k, v, f_logit, z_gmlp = jnp.split(z, in_pts, axis=-1)
        y_a = pool_mixer(z_pool, pool_w[l], pool_scale[l])
        y_b = conv_mixer(z_conv, conv_w[l], conv_b[l], conv_ln_g[l], conv_ln_b[l])
        y_c = forgetting_attention(q.reshape(B, S, ATTN_HEADS, ATTN_HEAD_DIM),
                                   k.reshape(B, S, ATTN_HEADS, ATTN_HEAD_DIM),
                                   v.reshape(B, S, ATTN_HEADS, ATTN_HEAD_DIM),
                                   f_logit + fgate_b[l])
        y_d = gmlp_mixer(z_gmlp, gmlp_ln_g[l], gmlp_ws[l], gmlp_bs[l])
        merged = jnp.zeros_like(x)
        for i, y in enumerate((y_a, y_b, y_c, y_d)):
            g = jax.nn.sigmoid(h @ w_gate[l][:, i * D:(i + 1) * D] + b_gate[l][i * D:(i + 1) * D])
            merged = merged + g * (y @ w_branch[l][br_pts[i]:br_pts[i + 1]])
        x = x + gt1[:, None, :] * (merged @ w_o[l])
        h2 = modulate(rms_norm(x, g_norm_ffn[l]), sh2, sc2)
        x = x + gt2[:, None, :] * moe_ffn(h2, router_w[l], router_b[l], w_up[l], b_up[l], w_down[l], b_down[l])
    return rms_norm(x, g_final)
```

```python
import functools
import math

import jax
import jax.numpy as jnp
from jax import lax
from jax.experimental import pallas as pl
from jax.experimental.pallas import tpu as pltpu

F32 = jnp.float32
BF16 = jnp.bfloat16
U32 = jnp.uint32
I32 = jnp.int32

EPS = 1e-5
POOL_WINDOWS = (2, 4, 8, 16)
CONV_KERNEL = 31
CONV_HALO = 32
GMLP_CHUNK = 128
TOP_K = 4
SWIGLU_LIMIT = 7.0
SWIGLU_ALPHA = 1.702
LANES = 128
MOE_ROWS = 256
VMEM_LIMIT = 56 * 1024 * 1024
HIGHEST = lax.Precision.HIGHEST


def _pick(n, target, mult):
    if n <= target:
        return n
    best = None
    for d in range(mult, target + 1, mult):
        if n % d == 0:
            best = d
    assert best is not None, (n, target, mult)
    return best


def _params(sem):
    return pltpu.CompilerParams(dimension_semantics=sem, vmem_limit_bytes=VMEM_LIMIT)


def _sigmoid(x):
    return 1.0 / (1.0 + jnp.exp(-x))


def _ada_kernel(c_ref, w_ref, b_ref, o_ref):
    c = c_ref[...]
    ca = c * _sigmoid(c)
    o_ref[0] = jnp.dot(ca, w_ref[0], preferred_element_type=F32, precision=HIGHEST) + b_ref[0]


def _ada(c_pad, w_ada, b_ada):
    L, D, N = w_ada.shape
    R = c_pad.shape[0]
    tn = _pick(N, 1024, LANES)
    return pl.pallas_call(
        _ada_kernel,
        out_shape=jax.ShapeDtypeStruct((L, R, N), F32),
        grid=(L, N // tn),
        in_specs=[pl.BlockSpec((R, D), lambda l, j: (0, 0)),
                  pl.BlockSpec((1, D, tn), lambda l, j: (l, 0, j)),
                  pl.BlockSpec((1, 1, tn), lambda l, j: (l, 0, j))],
        out_specs=pl.BlockSpec((1, R, tn), lambda l, j: (l, 0, j)),
        compiler_params=_params(("arbitrary", "arbitrary")),
        name="ada_mod",
    )(c_pad, w_ada, b_ada.reshape(L, 1, N))


def _norm_mod(x, g, sc, sh):
    var = jnp.mean(x * x, axis=-1, keepdims=True)
    y = x * lax.rsqrt(var + EPS) * g
    return y * (1.0 + sc) + sh


def _in_kernel(x_ref, g_ref, sh_ref, sc_ref, w_ref, wf_ref, h_ref, z_ref, f_ref):
    @pl.when(pl.program_id(1) == 0)
    def _():
        hb = _norm_mod(x_ref[...], g_ref[...], sc_ref[0], sh_ref[0]).astype(BF16)
        h_ref[...] = hb
        f_ref[...] = jnp.dot(hb, wf_ref[...], preferred_element_type=F32)

    z_ref[...] = jnp.dot(h_ref[...], w_ref[...], preferred_element_type=F32).astype(BF16)


def _in_proj(x2, g, sh, sc, w_main, w_f, S):
    T, D = x2.shape
    N = w_main.shape[1]
    tm = _pick(S, 512, 16)
    tn = _pick(N, 1408, LANES)
    tpb = S // tm
    return pl.pallas_call(
        _in_kernel,
        out_shape=(jax.ShapeDtypeStruct((T, D), BF16), jax.ShapeDtypeStruct((T, N), BF16),
                   jax.ShapeDtypeStruct((T, LANES), F32)),
        grid=(T // tm, N // tn),
        in_specs=[pl.BlockSpec((tm, D), lambda i, j: (i, 0)),
                  pl.BlockSpec((1, D), lambda i, j: (0, 0)),
                  pl.BlockSpec((1, 1, D), lambda i, j: (i // tpb, 0, 0)),
                  pl.BlockSpec((1, 1, D), lambda i, j: (i // tpb, 0, 0)),
                  pl.BlockSpec((D, tn), lambda i, j: (0, j)),
                  pl.BlockSpec((D, LANES), lambda i, j: (0, 0))],
        out_specs=(pl.BlockSpec((tm, D), lambda i, j: (i, 0)),
                   pl.BlockSpec((tm, tn), lambda i, j: (i, j)),
                   pl.BlockSpec((tm, LANES), lambda i, j: (i, 0))),
        compiler_params=_params(("arbitrary", "arbitrary")),
        name="in_proj",
    )(x2, g, sh, sc, w_main, w_f)


def _pool_kernel(z_ref, w_ref, s_ref, o_ref):
    S = z_ref.shape[1]
    G, GD, _ = w_ref.shape
    row = lax.broadcasted_iota(I32, (S, GD), 0)
    for g in range(G):
        wnd = POOL_WINDOWS[g]
        a = z_ref[0, :, g * GD:(g + 1) * GD].astype(F32)
        s = a
        k = 1
        while k < wnd:
            s = s + jnp.where(row >= k, pltpu.roll(s, k, axis=0), 0.0)
            k *= 2
        cnt = jnp.minimum(row + 1, wnd).astype(F32)
        mixed = s / cnt - a
        y = jnp.dot(mixed.astype(BF16), w_ref[g], preferred_element_type=F32)
        o_ref[0, :, g * GD:(g + 1) * GD] = (y * s_ref[:, g * GD:(g + 1) * GD]).astype(BF16)


def _pool(z3, pool_w, pool_scale):
    B, S, _ = z3.shape
    G, GD, _ = pool_w.shape
    P = G * GD
    return pl.pallas_call(
        _pool_kernel,
        out_shape=jax.ShapeDtypeStruct((B, S, P), BF16),
        grid=(B,),
        in_specs=[pl.BlockSpec((1, S, P), lambda b: (b, 0, 0)),
                  pl.BlockSpec((G, GD, GD), lambda b: (0, 0, 0)),
                  pl.BlockSpec((1, P), lambda b: (0, 0))],
        out_specs=pl.BlockSpec((1, S, P), lambda b: (b, 0, 0)),
        compiler_params=_params(("arbitrary",)),
        name="pool_mixer",
    )(z3, pool_w, pool_scale)


CONV_ROWS = 64


def _conv_kernel(za_ref, zb_ref, ha_ref, hb_ref, w_ref, b_ref, g_ref, beta_ref, o_ref, ext_ref):
    ts = za_ref.shape[1]
    C = za_ref.shape[2]
    halo = ha_ref[0].astype(F32) * _sigmoid(hb_ref[0].astype(F32))
    ext_ref[0:CONV_HALO, :] = jnp.where(pl.program_id(1) > 0, halo, 0.0)
    ext_ref[CONV_HALO:CONV_HALO + ts, :] = za_ref[0].astype(F32) * _sigmoid(zb_ref[0].astype(F32))
    first = CONV_HALO - (CONV_KERNEL - 1)
    R = min(CONV_ROWS, ts)
    for r0 in range(0, ts, R):
        acc = jnp.broadcast_to(b_ref[...], (R, C))
        for j in range(CONV_KERNEL):
            acc = acc + w_ref[j:j + 1, :] * ext_ref[r0 + first + j:r0 + first + j + R, :]
        mu = jnp.mean(acc, axis=-1, keepdims=True)
        xc = acc - mu
        var = jnp.mean(xc * xc, axis=-1, keepdims=True)
        y = xc * lax.rsqrt(var + EPS) * g_ref[...] + beta_ref[...]
        o_ref[0, r0:r0 + R, :] = (y * _sigmoid(y)).astype(BF16)


def _conv(z3, off_a, conv_w, conv_b, ln_g, ln_b):
    B, S, _ = z3.shape
    K, C = conv_w.shape
    ts = _pick(S, 512, CONV_HALO)
    ia, ib = off_a // C, off_a // C + 1
    hpt = ts // CONV_HALO
    assert off_a % C == 0
    halo_map = lambda col: (lambda b, i: (b, jnp.maximum(i * hpt - 1, 0), col))
    return pl.pallas_call(
        _conv_kernel,
        out_shape=jax.ShapeDtypeStruct((B, S, C), BF16),
        grid=(B, S // ts),
        in_specs=[pl.BlockSpec((1, ts, C), lambda b, i: (b, i, ia)),
                  pl.BlockSpec((1, ts, C), lambda b, i: (b, i, ib)),
                  pl.BlockSpec((1, CONV_HALO, C), halo_map(ia)),
                  pl.BlockSpec((1, CONV_HALO, C), halo_map(ib)),
                  pl.BlockSpec((K, C), lambda b, i: (0, 0)),
                  pl.BlockSpec((1, C), lambda b, i: (0, 0)),
                  pl.BlockSpec((1, C), lambda b, i: (0, 0)),
                  pl.BlockSpec((1, C), lambda b, i: (0, 0))],
        out_specs=pl.BlockSpec((1, ts, C), lambda b, i: (b, i, 0)),
        scratch_shapes=[pltpu.VMEM((CONV_HALO + ts, C), F32)],
        compiler_params=_params(("arbitrary", "arbitrary")),
        name="conv_mixer",
    )(z3, z3, z3, z3, conv_w, conv_b, ln_g, ln_b)


CUM_CHUNK = 256


def _cumf_kernel(f_ref, b_ref, fc_ref, fr_ref):
    S = f_ref.shape[1]
    H = fr_ref.shape[1]
    C = min(CUM_CHUNK, S)
    x = f_ref[0] + b_ref[...]
    ls = -(jnp.maximum(-x, 0.0) + jnp.log1p(jnp.exp(-jnp.abs(x))))
    tri = (lax.broadcasted_iota(I32, (C, C), 0) >= lax.broadcasted_iota(I32, (C, C), 1)).astype(F32)
    carry = jnp.zeros((1, LANES), F32)
    for c in range(S // C):
        cs = jnp.dot(tri, ls[c * C:(c + 1) * C], preferred_element_type=F32, precision=HIGHEST) + carry
        fc_ref[0, c * C:(c + 1) * C, :] = cs
        carry = cs[C - 1:C, :]
    fr_ref[0] = jnp.transpose(fc_ref[0])[0:H, :]


def _cumf(f3, fb_pad, H):
    B, S, _ = f3.shape
    return pl.pallas_call(
        _cumf_kernel,
        out_shape=(jax.ShapeDtypeStruct((B, S, LANES), F32), jax.ShapeDtypeStruct((B, H, S), F32)),
        grid=(B,),
        in_specs=[pl.BlockSpec((1, S, LANES), lambda b: (b, 0, 0)),
                  pl.BlockSpec((1, LANES), lambda b: (0, 0))],
        out_specs=(pl.BlockSpec((1, S, LANES), lambda b: (b, 0, 0)),
                   pl.BlockSpec((1, H, S), lambda b: (b, 0, 0))),
        compiler_params=_params(("arbitrary",)),
        name="forget_cumsum",
    )(f3, fb_pad)


def _attn_kernel(q_ref, k_ref, v_ref, fc_ref, fr_ref, o_ref, *, scale):
    h = pl.program_id(1)
    i = pl.program_id(2)
    tq, dh = q_ref.shape[1], q_ref.shape[2]
    q = q_ref[0]
    lane = lax.broadcasted_iota(I32, (tq, LANES), 1)
    fq = jnp.sum(jnp.where(lane == h, fc_ref[0], 0.0), axis=1, keepdims=True)

    def step(j, carry, masked):
        m, l, acc = carry
        start = pl.multiple_of(j * tq, tq)
        kj = k_ref[0, pl.ds(start, tq), :]
        vj = v_ref[0, pl.ds(start, tq), :]
        fk = fr_ref[0, pl.ds(h, 1), pl.ds(start, tq)]
        s = lax.dot_general(q, kj, (((1,), (1,)), ((), ())), preferred_element_type=F32)
        s = s * scale + fq - fk
        if masked:
            r = lax.broadcasted_iota(I32, (tq, tq), 0)
            c = lax.broadcasted_iota(I32, (tq, tq), 1)
            s = jnp.where(c <= r, s, -jnp.inf)
        m_new = jnp.maximum(m, jnp.max(s, axis=1, keepdims=True))
        p = jnp.exp(s - m_new)
        alpha = jnp.exp(m - m_new)
        l = alpha * l + jnp.sum(p, axis=1, keepdims=True)
        acc = alpha * acc + jnp.dot(p.astype(BF16), vj, preferred_element_type=F32)
        return m_new, l, acc

    init = (jnp.full((tq, 1), -jnp.inf, F32), jnp.zeros((tq, 1), F32), jnp.zeros((tq, dh), F32))
    carry = lax.fori_loop(0, i, lambda j, c: step(j, c, False), init)
    m, l, acc = step(i, carry, True)
    o_ref[0] = (acc / l).astype(BF16)


def _attn(z3, off_q, H, dh, fcol, frow):
    B, S, _ = z3.shape
    tq = _pick(S, 512, LANES)
    iq, ik, iv = off_q // dh, off_q // dh + H, off_q // dh + 2 * H
    assert off_q % dh == 0
    return pl.pallas_call(
        functools.partial(_attn_kernel, scale=1.0 / math.sqrt(dh)),
        out_shape=jax.ShapeDtypeStruct((B, S, H * dh), BF16),
        grid=(B, H, S // tq),
        in_specs=[pl.BlockSpec((1, tq, dh), lambda b, h, i: (b, i, iq + h)),
                  pl.BlockSpec((1, S, dh), lambda b, h, i: (b, 0, ik + h)),
                  pl.BlockSpec((1, S, dh), lambda b, h, i: (b, 0, iv + h)),
                  pl.BlockSpec((1, tq, LANES), lambda b, h, i: (b, i, 0)),
                  pl.BlockSpec((1, H, S), lambda b, h, i: (b, 0, 0))],
        out_specs=pl.BlockSpec((1, tq, dh), lambda b, h, i: (b, i, h)),
        compiler_params=_params(("arbitrary", "arbitrary", "arbitrary")),
        name="forget_attn",
    )(z3, z3, z3, fcol, frow)


def _gelu_tanh(x):
    return 0.5 * x * (1.0 + jnp.tanh(math.sqrt(2.0 / math.pi) * (x + 0.044715 * (x * x * x))))


def _gmlp_kernel(u_ref, v_ref, g_ref, ws_ref, bs_ref, o_ref):
    ts, GW = u_ref.shape
    G = ws_ref.shape[0]
    GD = GW // G
    ck = GMLP_CHUNK
    u = _gelu_tanh(u_ref[...].astype(F32))
    v = _gelu_tanh(v_ref[...].astype(F32))
    var = jnp.mean(v * v, axis=-1, keepdims=True)
    vb = (v * lax.rsqrt(var + EPS) * g_ref[...]).astype(BF16)
    causal = lax.broadcasted_iota(I32, (ck, ck), 0) >= lax.broadcasted_iota(I32, (ck, ck), 1)
    for g in range(G):
        wm = jnp.where(causal, ws_ref[g], 0.0).astype(BF16)
        for c in range(ts // ck):
            sv = jnp.dot(wm, vb[c * ck:(c + 1) * ck, g * GD:(g + 1) * GD], preferred_element_type=F32)
            sv = sv + bs_ref[g]
            o_ref[c * ck:(c + 1) * ck, g * GD:(g + 1) * GD] = (
                u[c * ck:(c + 1) * ck, g * GD:(g + 1) * GD] * sv).astype(BF16)


def _gmlp(z2, off_u, GW, ln_g, ws, bs_b):
    T, _ = z2.shape
    G = ws.shape[0]
    ts = _pick(T, 512, GMLP_CHUNK)
    iu = off_u // GW
    assert off_u % GW == 0
    return pl.pallas_call(
        _gmlp_kernel,
        out_shape=jax.ShapeDtypeStruct((T, GW), BF16),
        grid=(T // ts,),
        in_specs=[pl.BlockSpec((ts, GW), lambda i: (i, iu)),
                  pl.BlockSpec((ts, GW), lambda i: (i, iu + 1)),
                  pl.BlockSpec((1, GW), lambda i: (0, 0)),
                  pl.BlockSpec((G, GMLP_CHUNK, GMLP_CHUNK), lambda i: (0, 0, 0)),
                  pl.BlockSpec((G, GMLP_CHUNK, GW // G), lambda i: (0, 0, 0))],
        out_specs=pl.BlockSpec((ts, GW), lambda i: (i, 0)),
        compiler_params=_params(("arbitrary",)),
        name="gmlp_mixer",
    )(z2, z2, ln_g, ws, bs_b)


def _merge_kernel(h_ref, ya_ref, yb_ref, yc_ref, yd_ref, x_ref, gt_ref,
                  wg0_ref, wg1_ref, wg2_ref, wg3_ref, bg_ref,
                  wa_ref, wb_ref, wc_ref, wd_ref, wo_ref, o_ref, acc_ref):
    n = pl.program_id(1)
    h = h_ref[...]
    merged = None
    for br, (y_ref, wg_ref, wbr_ref) in enumerate(((ya_ref, wg0_ref, wa_ref), (yb_ref, wg1_ref, wb_ref),
                                                   (yc_ref, wg2_ref, wc_ref), (yd_ref, wg3_ref, wd_ref))):
        gate = _sigmoid(jnp.dot(h, wg_ref[...], preferred_element_type=F32) + bg_ref[br])
        term = gate * jnp.dot(y_ref[...], wbr_ref[...], preferred_element_type=F32)
        merged = term if merged is None else merged + term
    contrib = jnp.dot(merged.astype(BF16), wo_ref[...], preferred_element_type=F32)

    @pl.when(n == 0)
    def _():
        acc_ref[...] = contrib

    @pl.when(n > 0)
    def _():
        acc_ref[...] += contrib

    @pl.when(n == pl.num_programs(1) - 1)
    def _():
        o_ref[...] = x_ref[...] + gt_ref[0] * acc_ref[...]


def _merge(h, ya, yb, yc, yd, x2, gt, w_gate, b_gate4, wbr, w_o, S):
    T, D = x2.shape
    tm = _pick(S, 512, 16)
    tn = _pick(D, 256, LANES)
    nb = D // tn
    tpb = S // tm
    row = lambda width: pl.BlockSpec((tm, width), lambda i, n: (i, 0))
    wg = lambda br: pl.BlockSpec((D, tn), lambda i, n: (0, br * nb + n))
    wb = lambda rows: pl.BlockSpec((rows, tn), lambda i, n: (0, n))
    return pl.pallas_call(
        _merge_kernel,
        out_shape=jax.ShapeDtypeStruct((T, D), F32),
        grid=(T // tm, nb),
        in_specs=[row(D), row(ya.shape[1]), row(yb.shape[1]), row(yc.shape[1]), row(yd.shape[1]), row(D),
                  pl.BlockSpec((1, 1, D), lambda i, n: (i // tpb, 0, 0)),
                  wg(0), wg(1), wg(2), wg(3),
                  pl.BlockSpec((4, 1, tn), lambda i, n: (0, 0, n)),
                  wb(wbr[0].shape[0]), wb(wbr[1].shape[0]), wb(wbr[2].shape[0]), wb(wbr[3].shape[0]),
                  pl.BlockSpec((tn, D), lambda i, n: (n, 0))],
        out_specs=pl.BlockSpec((tm, D), lambda i, n: (i, 0)),
        scratch_shapes=[pltpu.VMEM((tm, D), F32)],
        compiler_params=_params(("arbitrary", "arbitrary")),
        name="gated_merge",
    )(h, ya, yb, yc, yd, x2, gt, w_gate, w_gate, w_gate, w_gate, b_gate4, *wbr, w_o)


def _pack_bf16_pairs(hb):
    W = hb.shape[1] // 2
    lo = pltpu.bitcast(hb[:, :W].astype(F32), U32)
    hi = pltpu.bitcast(hb[:, W:].astype(F32), U32)
    return (hi & jnp.uint32(0xFFFF0000)) | (lo >> 16)


def _unpack_bf16_pairs(p):
    lo = pltpu.bitcast(p << 16, F32)
    hi = pltpu.bitcast(p & jnp.uint32(0xFFFF0000), F32)
    return lo, hi


def _route_kernel(x_ref, g_ref, sh_ref, sc_ref, rw_ref, rb_ref, hp_ref, route_ref, gates_ref, cnt_ref, carry_ref):
    i = pl.program_id(0)
    tm = x_ref.shape[0]

    @pl.when(i == 0)
    def _():
        carry_ref[...] = jnp.zeros_like(carry_ref)

    hb = _norm_mod(x_ref[...], g_ref[...], sc_ref[0], sh_ref[0]).astype(BF16)
    hp_ref[...] = _pack_bf16_pairs(hb)
    logits = jnp.dot(hb, rw_ref[...], preferred_element_type=F32) + rb_ref[...]
    lane = lax.broadcasted_iota(I32, (tm, LANES), 1)
    lanef = lane.astype(F32)
    vals, sels = [], []
    for _ in range(TOP_K):
        m = jnp.max(logits, axis=1, keepdims=True)
        idx = jnp.min(jnp.where(logits == m, lanef, float(LANES)), axis=1, keepdims=True)
        sel = lanef == idx
        logits = jnp.where(sel, -jnp.inf, logits)
        vals.append(m)
        sels.append(sel)
    exps = [jnp.exp(v - vals[0]) for v in vals]
    denom = exps[0] + exps[1] + exps[2] + exps[3]
    onehot = jnp.zeros((tm, LANES), F32)
    for sel in sels:
        onehot = jnp.where(sel, 1.0, onehot)
    strict = (lax.broadcasted_iota(I32, (tm, tm), 0) > lax.broadcasted_iota(I32, (tm, tm), 1)).astype(BF16)
    before = jnp.dot(strict, onehot.astype(BF16), preferred_element_type=F32) + carry_ref[...]
    route = jnp.zeros((tm, LANES), F32)
    gates = jnp.zeros((tm, LANES), F32)
    for k in range(TOP_K):
        idx_k = jnp.sum(jnp.where(sels[k], lanef, 0.0), axis=1, keepdims=True)
        pos_k = jnp.sum(jnp.where(sels[k], before, 0.0), axis=1, keepdims=True)
        route = jnp.where(lane == k, idx_k, route)
        route = jnp.where(lane == TOP_K + k, pos_k, route)
        gates = jnp.where(lane == k, exps[k] / denom, gates)
    route_ref[...] = route.astype(I32)
    gates_ref[...] = gates
    carry_ref[...] += jnp.sum(onehot, axis=0, keepdims=True)
    cnt_ref[...] = carry_ref[...]


def _route(x2, g, sh, sc, rw_pad, rb_pad, S):
    T, D = x2.shape
    tm = _pick(S, 512, 16)
    tpb = S // tm
    return pl.pallas_call(
        _route_kernel,
        out_shape=(jax.ShapeDtypeStruct((T, D // 2), U32), jax.ShapeDtypeStruct((T, LANES), I32),
                   jax.ShapeDtypeStruct((T, LANES), F32), jax.ShapeDtypeStruct((1, LANES), F32)),
        grid=(T // tm,),
        in_specs=[pl.BlockSpec((tm, D), lambda i: (i, 0)),
                  pl.BlockSpec((1, D), lambda i: (0, 0)),
                  pl.BlockSpec((1, 1, D), lambda i: (i // tpb, 0, 0)),
                  pl.BlockSpec((1, 1, D), lambda i: (i // tpb, 0, 0)),
                  pl.BlockSpec((D, LANES), lambda i: (0, 0)),
                  pl.BlockSpec((1, LANES), lambda i: (0, 0))],
        out_specs=(pl.BlockSpec((tm, D // 2), lambda i: (i, 0)),
                   pl.BlockSpec((tm, LANES), lambda i: (i, 0)),
                   pl.BlockSpec((tm, LANES), lambda i: (i, 0)),
                   pl.BlockSpec((1, LANES), lambda i: (0, 0))),
        scratch_shapes=[pltpu.VMEM((1, LANES), F32)],
        compiler_params=_params(("arbitrary",)),
        name="moe_route",
    )(x2, g, sh, sc, rw_pad, rb_pad)


def _dispatch_kernel(dest_ref, h_ref, xs_ref, sem):
    tm = h_ref.shape[0]
    base = pl.program_id(0) * (tm * TOP_K)

    def row_copy(r, d):
        return pltpu.make_async_copy(h_ref.at[pl.ds(r, 1)], xs_ref.at[pl.ds(d, 1)], sem)

    def issue(r, _):
        for k in range(TOP_K):
            row_copy(r, dest_ref[base + r * TOP_K + k]).start()
        return 0

    def drain(r, _):
        for k in range(TOP_K):
            row_copy(r, dest_ref[base + r * TOP_K + k]).wait()
        return 0

    lax.fori_loop(0, tm, issue, 0)
    lax.fori_loop(0, tm, drain, 0)


def _dispatch(dest, hp, n_rows):
    T, W = hp.shape
    tm = _pick(T, 512, 8)
    return pl.pallas_call(
        _dispatch_kernel,
        out_shape=jax.ShapeDtypeStruct((n_rows, W), U32),
        grid_spec=pltpu.PrefetchScalarGridSpec(
            num_scalar_prefetch=1, grid=(T // tm,),
            in_specs=[pl.BlockSpec((tm, W), lambda i, d: (i, 0))],
            out_specs=pl.BlockSpec(memory_space=pl.ANY),
            scratch_shapes=[pltpu.SemaphoreType.DMA(())]),
        compiler_params=_params(("arbitrary",)),
        name="moe_dispatch",
    )(dest, hp)


def _expert_kernel(be_ref, nu_ref, x_ref, wu_ref, bu_ref, wd_ref, bd_ref, o_ref):
    @pl.when(pl.program_id(0) < nu_ref[0])
    def _():
        lo, hi = _unpack_bf16_pairs(x_ref[...])
        xb = jnp.concatenate([lo.astype(BF16), hi.astype(BF16)], axis=1)
        gu = jnp.dot(xb, wu_ref[0], preferred_element_type=F32) + bu_ref[0]
        F = gu.shape[1] // 2
        g = jnp.minimum(gu[:, :F], SWIGLU_LIMIT)
        lin = jnp.clip(gu[:, F:], -SWIGLU_LIMIT, SWIGLU_LIMIT)
        act = g * _sigmoid(SWIGLU_ALPHA * g) * (lin + 1.0)
        out = jnp.dot(act.astype(BF16), wd_ref[0], preferred_element_type=F32) + bd_ref[0]
        o_ref[...] = _pack_bf16_pairs(out.astype(BF16))


def _experts(block_e, n_used, xs, w_up, b_up, w_down, b_down):
    NP, W = xs.shape
    E, D, F2 = w_up.shape
    bm = MOE_ROWS
    blk = lambda i, be, nu: (jnp.minimum(i, nu[0] - 1), 0)
    return pl.pallas_call(
        _expert_kernel,
        out_shape=jax.ShapeDtypeStruct((NP, W), U32),
        grid_spec=pltpu.PrefetchScalarGridSpec(
            num_scalar_prefetch=2, grid=(NP // bm,),
            in_specs=[pl.BlockSpec((bm, W), blk),
                      pl.BlockSpec((1, D, F2), lambda i, be, nu: (be[i], 0, 0)),
                      pl.BlockSpec((1, 1, F2), lambda i, be, nu: (be[i], 0, 0)),
                      pl.BlockSpec((1, F2 // 2, D), lambda i, be, nu: (be[i], 0, 0)),
                      pl.BlockSpec((1, 1, D), lambda i, be, nu: (be[i], 0, 0))],
            out_specs=pl.BlockSpec((bm, W), blk)),
        compiler_params=_params(("arbitrary",)),
        name="moe_experts",
    )(block_e, n_used, xs, w_up, b_up.reshape(E, 1, F2), w_down, b_down.reshape(E, 1, D))


def _combine_kernel(dest_ref, x_ref, gates_ref, gt_ref, ys_ref, o_ref, buf_ref, sem):
    tm = x_ref.shape[0]
    base = pl.program_id(0) * (tm * TOP_K)

    def row_copy(r, k, d):
        return pltpu.make_async_copy(ys_ref.at[pl.ds(d, 1)], buf_ref.at[k, pl.ds(r, 1)], sem)

    def issue(r, _):
        for k in range(TOP_K):
            row_copy(r, k, dest_ref[base + r * TOP_K + k]).start()
        return 0

    def drain(r, _):
        for k in range(TOP_K):
            row_copy(r, k, dest_ref[base + r * TOP_K + k]).wait()
        return 0

    lax.fori_loop(0, tm, issue, 0)
    lax.fori_loop(0, tm, drain, 0)
    lane = lax.broadcasted_iota(I32, (tm, LANES), 1)
    gates = gates_ref[...]
    W = buf_ref.shape[2]
    y_lo = jnp.zeros((tm, W), F32)
    y_hi = jnp.zeros((tm, W), F32)
    for k in range(TOP_K):
        gk = jnp.sum(jnp.where(lane == k, gates, 0.0), axis=1, keepdims=True)
        lo, hi = _unpack_bf16_pairs(buf_ref[k])
        y_lo = y_lo + gk * lo
        y_hi = y_hi + gk * hi
    gt = gt_ref[0]
    o_ref[:, :W] = x_ref[:, :W] + gt[:, :W] * y_lo
    o_ref[:, W:] = x_ref[:, W:] + gt[:, W:] * y_hi


def _combine(dest, x2, gates, gt, ys, S):
    T, D = x2.shape
    tm = _pick(S, 256, 8)
    tpb = S // tm
    return pl.pallas_call(
        _combine_kernel,
        out_shape=jax.ShapeDtypeStruct((T, D), F32),
        grid_spec=pltpu.PrefetchScalarGridSpec(
            num_scalar_prefetch=1, grid=(T // tm,),
            in_specs=[pl.BlockSpec((tm, D), lambda i, d: (i, 0)),
                      pl.BlockSpec((tm, LANES), lambda i, d: (i, 0)),
                      pl.BlockSpec((1, 1, D), lambda i, d: (i // tpb, 0, 0)),
                      pl.BlockSpec(memory_space=pl.ANY)],
            out_specs=pl.BlockSpec((tm, D), lambda i, d: (i, 0)),
            scratch_shapes=[pltpu.VMEM((TOP_K, tm, D // 2), U32), pltpu.SemaphoreType.DMA(())]),
        compiler_params=_params(("arbitrary",)),
        name="moe_combine",
    )(dest, x2, gates, gt, ys)


def _final_kernel(x_ref, g_ref, o_ref):
    x = x_ref[...]
    var = jnp.mean(x * x, axis=-1, keepdims=True)
    o_ref[...] = x * lax.rsqrt(var + EPS) * g_ref[...]


def _final_norm(x2, g):
    T, D = x2.shape
    tm = _pick(T, 512, 8)
    return pl.pallas_call(
        _final_kernel,
        out_shape=jax.ShapeDtypeStruct((T, D), F32),
        grid=(T // tm,),
        in_specs=[pl.BlockSpec((tm, D), lambda i: (i, 0)), pl.BlockSpec((1, D), lambda i: (0, 0))],
        out_specs=pl.BlockSpec((tm, D), lambda i: (i, 0)),
        compiler_params=_params(("arbitrary",)),
        name="final_norm",
    )(x2, g)


def _routing_tables(route, counts, n_experts, n_blocks):
    bm = MOE_ROWS
    cnt = counts[0, :n_experts].astype(I32)
    padded = (cnt + bm - 1) // bm * bm
    pend = jnp.cumsum(padded)
    pstart = pend - padded
    dest = (pstart[route[:, :TOP_K]] + route[:, TOP_K:2 * TOP_K]).reshape(-1)
    block_e = jnp.minimum(jnp.searchsorted(pend, jnp.arange(n_blocks, dtype=I32) * bm, side='right'),
                          n_experts - 1).astype(I32)
    n_used = (pend[-1:] // bm).astype(I32)
    return dest, block_e, n_used


def kernel(x, c, w_ada, b_ada, g_norm_mix, w_in, pool_w, pool_scale, conv_w, conv_b, conv_ln_g, conv_ln_b,
           fgate_b, gmlp_ln_g, gmlp_ws, gmlp_bs, w_gate, b_gate, w_branch, w_o, g_norm_ffn, router_w, router_b,
           w_up, b_up, w_down, b_down, g_final):
    B, S, D = x.shape
    L = w_ada.shape[0]
    T = B * S
    G, GD = pool_w.shape[1], pool_w.shape[2]
    P = G * GD
    C = conv_w.shape[2]
    H = fgate_b.shape[1]
    GW = gmlp_ln_g.shape[1]
    A = (w_in.shape[2] - P - 2 * C - H - 2 * GW) // 3
    dh = A // H
    E = router_w.shape[2]
    off_conv, off_q = P, P + 2 * C
    off_f = off_q + 3 * A
    off_gmlp = off_q + 3 * A
    n_blocks = -(-(T * TOP_K + E * (MOE_ROWS - 1)) // MOE_ROWS)

    c_pad = jnp.pad(c, ((0, (-B) % 8), (0, 0)))
    mod = _ada(c_pad, w_ada, b_ada)
    x2 = x.reshape(T, D)

    for l in range(L):
        sh1, sc1, gt1, sh2, sc2, gt2 = (mod[l, :B, i * D:(i + 1) * D].reshape(B, 1, D) for i in range(6))
        w_main = jnp.concatenate([w_in[l][:, :off_f], w_in[l][:, off_f + H:]], axis=1).astype(BF16)
        w_f = jnp.pad(w_in[l][:, off_f:off_f + H], ((0, 0), (0, LANES - H))).astype(BF16)
        h, z, f = _in_proj(x2, g_norm_mix[l].reshape(1, D), sh1, sc1, w_main, w_f, S)
        z3 = z.reshape(B, S, -1)

        y_a = _pool(z3, pool_w[l].astype(BF16), pool_scale[l].reshape(1, P))
        y_b = _conv(z3, off_conv, conv_w[l], conv_b[l].reshape(1, C), conv_ln_g[l].reshape(1, C),
                    conv_ln_b[l].reshape(1, C))
        fb_pad = jnp.pad(fgate_b[l], (0, LANES - H)).reshape(1, LANES)
        fcol, frow = _cumf(f.reshape(B, S, LANES), fb_pad, H)
        y_c = _attn(z3, off_q, H, dh, fcol, frow)
        bs_b = jnp.broadcast_to(gmlp_bs[l][:, :, None], (G, GMLP_CHUNK, GW // G))
        y_d = _gmlp(z, off_gmlp, GW, gmlp_ln_g[l].reshape(1, GW), gmlp_ws[l], bs_b)

        wb = w_branch[l].astype(BF16)
        wbr = (wb[:P], wb[P:P + C], wb[P + C:P + C + A], wb[P + C + A:])
        x2 = _merge(h, y_a.reshape(T, P), y_b.reshape(T, C), y_c.reshape(T, A), y_d, x2, gt1,
                    w_gate[l].astype(BF16), b_gate[l].reshape(4, 1, D), wbr, w_o[l].astype(BF16), S)

        rw_pad = jnp.pad(router_w[l], ((0, 0), (0, LANES - E))).astype(BF16)
        rb_pad = jnp.pad(router_b[l], (0, LANES - E), constant_values=-1e30).reshape(1, LANES)
        hp, route, gates, counts = _route(x2, g_norm_ffn[l].reshape(1, D), sh2, sc2, rw_pad, rb_pad, S)
        dest, block_e, n_used = _routing_tables(route, counts, E, n_blocks)
        xs = _dispatch(dest, hp, n_blocks * MOE_ROWS)
        ys = _experts(block_e, n_used, xs, w_up[l].astype(BF16), b_up[l], w_down[l].astype(BF16), b_down[l])
        x2 = _combine(dest, x2, gates, gt2, ys, S)

    return _final_norm(x2, g_final.reshape(1, D)).reshape(B, S, D)
```

```python
import functools
import math

import jax
import jax.numpy as jnp
from jax import lax
from jax.experimental import pallas as pl
from jax.experimental.pallas import tpu as pltpu

F32 = jnp.float32
BF16 = jnp.bfloat16
U32 = jnp.uint32
I32 = jnp.int32

EPS = 1e-5
POOL_WINDOWS = (2, 4, 8, 16)
CONV_KERNEL = 31
CONV_HALO = 32
GMLP_CHUNK = 128
TOP_K = 4
SWIGLU_LIMIT = 7.0
SWIGLU_ALPHA = 1.702
LANES = 128
SUBLANES = 8
MOE_ROWS = 256
VMEM_LIMIT = 56 * 1024 * 1024
HIGHEST = lax.Precision.HIGHEST


def _pick(n, target, mult):
    if n <= target:
        return n
    best = None
    for d in range(mult, target + 1, mult):
        if n % d == 0:
            best = d
    assert best is not None, (n, target, mult)
    return best


def _params(sem):
    return pltpu.CompilerParams(dimension_semantics=sem, vmem_limit_bytes=VMEM_LIMIT)


def _sigmoid(x):
    return 1.0 / (1.0 + jnp.exp(-x))


def _ada_kernel(c_ref, w_ref, b_ref, o_ref):
    c = c_ref[...]
    ca = c * _sigmoid(c)
    o_ref[0] = jnp.dot(ca, w_ref[0], preferred_element_type=F32, precision=HIGHEST) + b_ref[0]


def _ada(c_pad, w_ada, b_ada):
    L, D, N = w_ada.shape
    R = c_pad.shape[0]
    tn = _pick(N, 1024, LANES)
    return pl.pallas_call(
        _ada_kernel,
        out_shape=jax.ShapeDtypeStruct((L, R, N), F32),
        grid=(L, N // tn),
        in_specs=[pl.BlockSpec((R, D), lambda l, j: (0, 0)),
                  pl.BlockSpec((1, D, tn), lambda l, j: (l, 0, j)),
                  pl.BlockSpec((1, 1, tn), lambda l, j: (l, 0, j))],
        out_specs=pl.BlockSpec((1, R, tn), lambda l, j: (l, 0, j)),
        compiler_params=_params(("arbitrary", "arbitrary")),
        name="ada_mod",
    )(c_pad, w_ada, b_ada.reshape(L, 1, N))


def _norm_mod(x, g, sc, sh):
    var = jnp.mean(x * x, axis=-1, keepdims=True)
    y = x * lax.rsqrt(var + EPS) * g
    return y * (1.0 + sc) + sh


def _in_kernel(x_ref, g_ref, sh_ref, sc_ref, w_ref, wf_ref, h_ref, z_ref, f_ref):
    @pl.when(pl.program_id(1) == 0)
    def _():
        hb = _norm_mod(x_ref[...], g_ref[...], sc_ref[0], sh_ref[0]).astype(BF16)
        h_ref[...] = hb
        f_ref[...] = jnp.dot(hb, wf_ref[...], preferred_element_type=F32)

    z_ref[...] = jnp.dot(h_ref[...], w_ref[...], preferred_element_type=F32).astype(BF16)


def _in_proj(x2, g, sh, sc, w_main, w_f, S):
    T, D = x2.shape
    N = w_main.shape[1]
    tm = _pick(S, 512, 16)
    tn = _pick(N, 1408, LANES)
    tpb = S // tm
    return pl.pallas_call(
        _in_kernel,
        out_shape=(jax.ShapeDtypeStruct((T, D), BF16), jax.ShapeDtypeStruct((T, N), BF16),
                   jax.ShapeDtypeStruct((T, LANES), F32)),
        grid=(T // tm, N // tn),
        in_specs=[pl.BlockSpec((tm, D), lambda i, j: (i, 0)),
                  pl.BlockSpec((1, D), lambda i, j: (0, 0)),
                  pl.BlockSpec((1, 1, D), lambda i, j: (i // tpb, 0, 0)),
                  pl.BlockSpec((1, 1, D), lambda i, j: (i // tpb, 0, 0)),
                  pl.BlockSpec((D, tn), lambda i, j: (0, j)),
                  pl.BlockSpec((D, LANES), lambda i, j: (0, 0))],
        out_specs=(pl.BlockSpec((tm, D), lambda i, j: (i, 0)),
                   pl.BlockSpec((tm, tn), lambda i, j: (i, j)),
                   pl.BlockSpec((tm, LANES), lambda i, j: (i, 0))),
        compiler_params=_params(("arbitrary", "arbitrary")),
        name="in_proj",
    )(x2, g, sh, sc, w_main, w_f)


def _pool_kernel(z_ref, w_ref, s_ref, o_ref):
    S = z_ref.shape[1]
    G, GD, _ = w_ref.shape
    row = lax.broadcasted_iota(I32, (S, GD), 0)
    for g in range(G):
        wnd = POOL_WINDOWS[g]
        a = z_ref[0, :, g * GD:(g + 1) * GD].astype(F32)
        s = a
        k = 1
        while k < wnd:
            s = s + jnp.where(row >= k, pltpu.roll(s, k, axis=0), 0.0)
            k *= 2
        cnt = jnp.minimum(row + 1, wnd).astype(F32)
        mixed = s / cnt - a
        y = jnp.dot(mixed.astype(BF16), w_ref[g], preferred_element_type=F32)
        o_ref[0, :, g * GD:(g + 1) * GD] = (y * s_ref[:, g * GD:(g + 1) * GD]).astype(BF16)


def _pool(z3, pool_w, pool_scale):
    B, S, _ = z3.shape
    G, GD, _ = pool_w.shape
    P = G * GD
    return pl.pallas_call(
        _pool_kernel,
        out_shape=jax.ShapeDtypeStruct((B, S, P), BF16),
        grid=(B,),
        in_specs=[pl.BlockSpec((1, S, P), lambda b: (b, 0, 0)),
                  pl.BlockSpec((G, GD, GD), lambda b: (0, 0, 0)),
                  pl.BlockSpec((1, P), lambda b: (0, 0))],
        out_specs=pl.BlockSpec((1, S, P), lambda b: (b, 0, 0)),
        compiler_params=_params(("arbitrary",)),
        name="pool_mixer",
    )(z3, pool_w, pool_scale)


CONV_ROWS = 64


def _conv_kernel(za_ref, zb_ref, ha_ref, hb_ref, w_ref, b_ref, g_ref, beta_ref, o_ref, ext_ref, sh_ref):
    ts = za_ref.shape[1]
    C = za_ref.shape[2]
    halo = ha_ref[0].astype(F32) * _sigmoid(hb_ref[0].astype(F32))
    ext_ref[0:CONV_HALO, :] = jnp.where(pl.program_id(1) > 0, halo, 0.0)
    ext_ref[CONV_HALO:CONV_HALO + ts, :] = za_ref[0].astype(F32) * _sigmoid(zb_ref[0].astype(F32))
    n_sh = CONV_HALO + ts - SUBLANES
    sh_ref[0] = ext_ref[...]
    for b in range(1, SUBLANES):
        sh_ref[b, 0:n_sh, :] = ext_ref[b:b + n_sh, :]
    first = CONV_HALO - (CONV_KERNEL - 1)
    R = min(CONV_ROWS, ts)
    for r0 in range(0, ts, R):
        acc = jnp.broadcast_to(b_ref[...], (R, C))
        for j in range(CONV_KERNEL):
            a, b = divmod(first + j, SUBLANES)
            acc = acc + w_ref[j:j + 1, :] * sh_ref[b, r0 + a * SUBLANES:r0 + a * SUBLANES + R, :]
        mu = jnp.mean(acc, axis=-1, keepdims=True)
        xc = acc - mu
        var = jnp.mean(xc * xc, axis=-1, keepdims=True)
        y = xc * lax.rsqrt(var + EPS) * g_ref[...] + beta_ref[...]
        o_ref[0, r0:r0 + R, :] = (y * _sigmoid(y)).astype(BF16)


def _conv(z3, off_a, conv_w, conv_b, ln_g, ln_b):
    B, S, _ = z3.shape
    K, C = conv_w.shape
    ts = _pick(S, 512, CONV_HALO)
    ia, ib = off_a // C, off_a // C + 1
    hpt = ts // CONV_HALO
    assert off_a % C == 0
    halo_map = lambda col: (lambda b, i: (b, jnp.maximum(i * hpt - 1, 0), col))
    return pl.pallas_call(
        _conv_kernel,
        out_shape=jax.ShapeDtypeStruct((B, S, C), BF16),
        grid=(B, S // ts),
        in_specs=[pl.BlockSpec((1, ts, C), lambda b, i: (b, i, ia)),
                  pl.BlockSpec((1, ts, C), lambda b, i: (b, i, ib)),
                  pl.BlockSpec((1, CONV_HALO, C), halo_map(ia)),
                  pl.BlockSpec((1, CONV_HALO, C), halo_map(ib)),
                  pl.BlockSpec((K, C), lambda b, i: (0, 0)),
                  pl.BlockSpec((1, C), lambda b, i: (0, 0)),
                  pl.BlockSpec((1, C), lambda b, i: (0, 0)),
                  pl.BlockSpec((1, C), lambda b, i: (0, 0))],
        out_specs=pl.BlockSpec((1, ts, C), lambda b, i: (b, i, 0)),
        scratch_shapes=[pltpu.VMEM((CONV_HALO + ts, C), F32),
                        pltpu.VMEM((SUBLANES, CONV_HALO + ts, C), F32)],
        compiler_params=_params(("arbitrary", "arbitrary")),
        name="conv_mixer",
    )(z3, z3, z3, z3, conv_w, conv_b, ln_g, ln_b)


CUM_CHUNK = 256


def _cumf_kernel(f_ref, b_ref, fc_ref):
    S = f_ref.shape[1]
    C = min(CUM_CHUNK, S)
    x = f_ref[0] + b_ref[...]
    ls = -(jnp.maximum(-x, 0.0) + jnp.log1p(jnp.exp(-jnp.abs(x))))
    tri = (lax.broadcasted_iota(I32, (C, C), 0) >= lax.broadcasted_iota(I32, (C, C), 1)).astype(F32)
    carry = jnp.zeros((1, LANES), F32)
    for c in range(S // C):
        cs = jnp.dot(tri, ls[c * C:(c + 1) * C], preferred_element_type=F32, precision=HIGHEST) + carry
        fc_ref[0, c * C:(c + 1) * C, :] = cs
        carry = cs[C - 1:C, :]


def _cumf(f3, fb_pad):
    B, S, _ = f3.shape
    return pl.pallas_call(
        _cumf_kernel,
        out_shape=jax.ShapeDtypeStruct((B, S, LANES), F32),
        grid=(B,),
        in_specs=[pl.BlockSpec((1, S, LANES), lambda b: (b, 0, 0)),
                  pl.BlockSpec((1, LANES), lambda b: (0, 0))],
        out_specs=pl.BlockSpec((1, S, LANES), lambda b: (b, 0, 0)),
        compiler_params=_params(("arbitrary",)),
        name="forget_cumsum",
    )(f3, fb_pad)


ATTN_STRIP = 32
LOG2E = 1.4426950408889634


def _split3(x):
    hi = x.astype(BF16).astype(F32)
    mid = (x - hi).astype(BF16).astype(F32)
    return hi, mid, x - hi - mid


def _attn_kernel(q_ref, k_ref, v_ref, fc_ref, o_ref, ka_ref, fh_ref, qa_ref, s_ref, p_ref, m_ref, l_ref, acc_ref,
                 *, scale):
    h = pl.program_id(1)
    i = pl.program_id(2)
    tq = q_ref.shape[1]
    S = k_ref.shape[1]
    nl = tq // LANES
    R = ATTN_STRIP
    q_start = pl.multiple_of(i * tq, tq)

    @pl.when(i == 0)
    def _():
        lane = lax.broadcasted_iota(I32, (S, LANES), 1)
        fh = jnp.sum(jnp.where(lane == h, fc_ref[0], 0.0), axis=1, keepdims=True) * LOG2E
        fh_ref[...] = jnp.broadcast_to(fh, (S, LANES))
        hi, mid, lo = _split3(fh)
        aug = jnp.where(lane == 0, hi, jnp.where(lane == 1, mid, jnp.where(lane == 2, lo,
                                                                           jnp.where(lane < 6, 1.0, 0.0))))
        ka_ref[:, 0:LANES] = k_ref[0]
        ka_ref[:, LANES:2 * LANES] = aug.astype(BF16)

    hi0, mid0, lo0 = _split3(fh_ref[pl.ds(q_start, SUBLANES), :][0:1, :])
    lane1 = lax.broadcasted_iota(I32, (1, LANES), 1)
    qrow = jnp.where(lane1 < 3, -1.0, jnp.where(lane1 == 3, hi0, jnp.where(lane1 == 4, mid0,
                                                                           jnp.where(lane1 == 5, lo0, 0.0))))
    qa_ref[:, 0:LANES] = (q_ref[0].astype(F32) * (scale * LOG2E)).astype(BF16)
    qa_ref[:, LANES:2 * LANES] = jnp.broadcast_to(qrow, (tq, LANES)).astype(BF16)
    m_ref[...] = jnp.full(m_ref.shape, -jnp.inf, F32)
    l_ref[...] = jnp.zeros(l_ref.shape, F32)
    acc_ref[...] = jnp.zeros(acc_ref.shape, F32)

    def logits(j, slot):
        start = pl.multiple_of(j * tq, tq)
        s_ref[slot] = lax.dot_general(qa_ref[...], ka_ref[pl.ds(start, tq), :], (((1,), (1,)), ((), ())),
                                      preferred_element_type=F32)

    def softmax_pv(j, slot, masked):
        start = pl.multiple_of(j * tq, tq)

        def scaled(r0, c):
            t = s_ref[slot, r0:r0 + R, c * LANES:(c + 1) * LANES]
            if masked:
                row = lax.broadcasted_iota(I32, (R, LANES), 0) + r0
                col = lax.broadcasted_iota(I32, (R, LANES), 1) + c * LANES
                t = jnp.where(col <= row, t, -jnp.inf)
            return t

        def live(r0):
            return [c for c in range(nl) if not masked or c * LANES < r0 + R]

        for r0 in range(0, tq, R):
            mx = scaled(r0, 0)
            for c in live(r0)[1:]:
                mx = jnp.maximum(mx, scaled(r0, c))
            m_old = m_ref[r0:r0 + R, :]
            m_new = jnp.maximum(m_old, jnp.max(mx, axis=1, keepdims=True))
            alpha = jnp.exp2(m_old - m_new)
            m_ref[r0:r0 + R, :] = m_new
            l_ref[r0:r0 + R, :] = alpha * l_ref[r0:r0 + R, :]
            acc_ref[r0:r0 + R, :] = alpha * acc_ref[r0:r0 + R, :]
        for r0 in range(0, tq, R):
            m_new = m_ref[r0:r0 + R, :]
            rs = None
            for c in range(nl):
                if c in live(r0):
                    p = jnp.exp2(scaled(r0, c) - m_new)
                    rs = p if rs is None else rs + p
                else:
                    p = jnp.zeros((R, LANES), F32)
                p_ref[r0:r0 + R, c * LANES:(c + 1) * LANES] = p.astype(BF16)
            l_ref[r0:r0 + R, :] += rs
        acc_ref[...] += jnp.dot(p_ref[...], v_ref[0, pl.ds(start, tq), :], preferred_element_type=F32)

    logits(0, 0)

    def pair(k, carry):
        logits(2 * k + 1, 1)
        softmax_pv(2 * k, 0, False)
        logits(2 * k + 2, 0)
        softmax_pv(2 * k + 1, 1, False)
        return carry

    lax.fori_loop(0, i // 2, pair, 0)

    @pl.when(i % 2 == 1)
    def _():
        logits(i, 1)
        softmax_pv(i - 1, 0, False)
        softmax_pv(i, 1, True)

    @pl.when(i % 2 == 0)
    def _():
        softmax_pv(i, 0, True)

    o_ref[0] = (acc_ref[...] / jnp.sum(l_ref[...], axis=1, keepdims=True)).astype(BF16)


def _attn(z3, off_q, H, dh, fcol):
    B, S, _ = z3.shape
    tq = _pick(S, 512, LANES)
    iq, ik, iv = off_q // dh, off_q // dh + H, off_q // dh + 2 * H
    assert off_q % dh == 0 and dh == LANES and tq % ATTN_STRIP == 0
    return pl.pallas_call(
        functools.partial(_attn_kernel, scale=1.0 / math.sqrt(dh)),
        out_shape=jax.ShapeDtypeStruct((B, S, H * dh), BF16),
        grid=(B, H, S // tq),
        in_specs=[pl.BlockSpec((1, tq, dh), lambda b, h, i: (b, i, iq + h)),
                  pl.BlockSpec((1, S, dh), lambda b, h, i: (b, 0, ik + h)),
                  pl.BlockSpec((1, S, dh), lambda b, h, i: (b, 0, iv + h)),
                  pl.BlockSpec((1, S, LANES), lambda b, h, i: (b, 0, 0))],
        out_specs=pl.BlockSpec((1, tq, dh), lambda b, h, i: (b, i, h)),
        scratch_shapes=[pltpu.VMEM((S, 2 * LANES), BF16), pltpu.VMEM((S, LANES), F32),
                        pltpu.VMEM((tq, 2 * LANES), BF16), pltpu.VMEM((2, tq, tq), F32),
                        pltpu.VMEM((tq, tq), BF16), pltpu.VMEM((tq, LANES), F32), pltpu.VMEM((tq, LANES), F32),
                        pltpu.VMEM((tq, dh), F32)],
        compiler_params=_params(("arbitrary", "arbitrary", "arbitrary")),
        name="forget_attn",
    )(z3, z3, z3, fcol)


def _gelu_tanh(x):
    return 0.5 * x * (1.0 + jnp.tanh(math.sqrt(2.0 / math.pi) * (x + 0.044715 * (x * x * x))))


def _gmlp_kernel(u_ref, v_ref, g_ref, ws_ref, bs_ref, o_ref):
    ts, GW = u_ref.shape
    G = ws_ref.shape[0]
    GD = GW // G
    ck = GMLP_CHUNK
    u = _gelu_tanh(u_ref[...].astype(F32))
    v = _gelu_tanh(v_ref[...].astype(F32))
    var = jnp.mean(v * v, axis=-1, keepdims=True)
    vb = (v * lax.rsqrt(var + EPS) * g_ref[...]).astype(BF16)
    causal = lax.broadcasted_iota(I32, (ck, ck), 0) >= lax.broadcasted_iota(I32, (ck, ck), 1)
    for g in range(G):
        wm = jnp.where(causal, ws_ref[g], 0.0).astype(BF16)
        for c in range(ts // ck):
            sv = jnp.dot(wm, vb[c * ck:(c + 1) * ck, g * GD:(g + 1) * GD], preferred_element_type=F32)
            sv = sv + bs_ref[g]
            o_ref[c * ck:(c + 1) * ck, g * GD:(g + 1) * GD] = (
                u[c * ck:(c + 1) * ck, g * GD:(g + 1) * GD] * sv).astype(BF16)


def _gmlp(z2, off_u, GW, ln_g, ws, bs_b):
    T, _ = z2.shape
    G = ws.shape[0]
    ts = _pick(T, 512, GMLP_CHUNK)
    iu = off_u // GW
    assert off_u % GW == 0
    return pl.pallas_call(
        _gmlp_kernel,
        out_shape=jax.ShapeDtypeStruct((T, GW), BF16),
        grid=(T // ts,),
        in_specs=[pl.BlockSpec((ts, GW), lambda i: (i, iu)),
                  pl.BlockSpec((ts, GW), lambda i: (i, iu + 1)),
                  pl.BlockSpec((1, GW), lambda i: (0, 0)),
                  pl.BlockSpec((G, GMLP_CHUNK, GMLP_CHUNK), lambda i: (0, 0, 0)),
                  pl.BlockSpec((G, GMLP_CHUNK, GW // G), lambda i: (0, 0, 0))],
        out_specs=pl.BlockSpec((ts, GW), lambda i: (i, 0)),
        compiler_params=_params(("arbitrary",)),
        name="gmlp_mixer",
    )(z2, z2, ln_g, ws, bs_b)


def _merge_kernel(h_ref, ya_ref, yb_ref, yc_ref, yd_ref, x_ref, gt_ref,
                  wg0_ref, wg1_ref, wg2_ref, wg3_ref, bg_ref,
                  wa_ref, wb_ref, wc_ref, wd_ref, wo_ref, o_ref, acc_ref):
    n = pl.program_id(1)
    h = h_ref[...]
    merged = None
    for br, (y_ref, wg_ref, wbr_ref) in enumerate(((ya_ref, wg0_ref, wa_ref), (yb_ref, wg1_ref, wb_ref),
                                                   (yc_ref, wg2_ref, wc_ref), (yd_ref, wg3_ref, wd_ref))):
        gate = _sigmoid(jnp.dot(h, wg_ref[...], preferred_element_type=F32) + bg_ref[br])
        term = gate * jnp.dot(y_ref[...], wbr_ref[...], preferred_element_type=F32)
        merged = term if merged is None else merged + term
    contrib = jnp.dot(merged.astype(BF16), wo_ref[...], preferred_element_type=F32)

    @pl.when(n == 0)
    def _():
        acc_ref[...] = contrib

    @pl.when(n > 0)
    def _():
        acc_ref[...] += contrib

    @pl.when(n == pl.num_programs(1) - 1)
    def _():
        o_ref[...] = x_ref[...] + gt_ref[0] * acc_ref[...]


def _merge(h, ya, yb, yc, yd, x2, gt, w_gate, b_gate4, wbr, w_o, S):
    T, D = x2.shape
    tm = _pick(S, 512, 16)
    tn = _pick(D, 256, LANES)
    nb = D // tn
    tpb = S // tm
    row = lambda width: pl.BlockSpec((tm, width), lambda i, n: (i, 0))
    wg = lambda br: pl.BlockSpec((D, tn), lambda i, n: (0, br * nb + n))
    wb = lambda rows: pl.BlockSpec((rows, tn), lambda i, n: (0, n))
    return pl.pallas_call(
        _merge_kernel,
        out_shape=jax.ShapeDtypeStruct((T, D), F32),
        grid=(T // tm, nb),
        in_specs=[row(D), row(ya.shape[1]), row(yb.shape[1]), row(yc.shape[1]), row(yd.shape[1]), row(D),
                  pl.BlockSpec((1, 1, D), lambda i, n: (i // tpb, 0, 0)),
                  wg(0), wg(1), wg(2), wg(3),
                  pl.BlockSpec((4, 1, tn), lambda i, n: (0, 0, n)),
                  wb(wbr[0].shape[0]), wb(wbr[1].shape[0]), wb(wbr[2].shape[0]), wb(wbr[3].shape[0]),
                  pl.BlockSpec((tn, D), lambda i, n: (n, 0))],
        out_specs=pl.BlockSpec((tm, D), lambda i, n: (i, 0)),
        scratch_shapes=[pltpu.VMEM((tm, D), F32)],
        compiler_params=_params(("arbitrary", "arbitrary")),
        name="gated_merge",
    )(h, ya, yb, yc, yd, x2, gt, w_gate, w_gate, w_gate, w_gate, b_gate4, *wbr, w_o)


def _pack_bf16_pairs(hb):
    W = hb.shape[1] // 2
    lo = pltpu.bitcast(hb[:, :W].astype(F32), U32)
    hi = pltpu.bitcast(hb[:, W:].astype(F32), U32)
    return (hi & jnp.uint32(0xFFFF0000)) | (lo >> 16)


def _unpack_bf16_pairs(p):
    lo = pltpu.bitcast(p << 16, F32)
    hi = pltpu.bitcast(p & jnp.uint32(0xFFFF0000), F32)
    return lo, hi


def _route_kernel(x_ref, g_ref, sh_ref, sc_ref, rw_ref, rb_ref, hp_ref, route_ref, gates_ref, cnt_ref, carry_ref):
    i = pl.program_id(0)
    tm = x_ref.shape[0]

    @pl.when(i == 0)
    def _():
        carry_ref[...] = jnp.zeros_like(carry_ref)

    hb = _norm_mod(x_ref[...], g_ref[...], sc_ref[0], sh_ref[0]).astype(BF16)
    hp_ref[...] = _pack_bf16_pairs(hb)
    logits = jnp.dot(hb, rw_ref[...], preferred_element_type=F32) + rb_ref[...]
    lane = lax.broadcasted_iota(I32, (tm, LANES), 1)
    lanef = lane.astype(F32)
    vals, sels = [], []
    for _ in range(TOP_K):
        m = jnp.max(logits, axis=1, keepdims=True)
        idx = jnp.min(jnp.where(logits == m, lanef, float(LANES)), axis=1, keepdims=True)
        sel = lanef == idx
        logits = jnp.where(sel, -jnp.inf, logits)
        vals.append(m)
        sels.append(sel)
    exps = [jnp.exp(v - vals[0]) for v in vals]
    denom = exps[0] + exps[1] + exps[2] + exps[3]
    onehot = jnp.zeros((tm, LANES), F32)
    for sel in sels:
        onehot = jnp.where(sel, 1.0, onehot)
    strict = (lax.broadcasted_iota(I32, (tm, tm), 0) > lax.broadcasted_iota(I32, (tm, tm), 1)).astype(BF16)
    before = jnp.dot(strict, onehot.astype(BF16), preferred_element_type=F32) + carry_ref[...]
    route = jnp.zeros((tm, LANES), F32)
    gates = jnp.zeros((tm, LANES), F32)
    for k in range(TOP_K):
        idx_k = jnp.sum(jnp.where(sels[k], lanef, 0.0), axis=1, keepdims=True)
        pos_k = jnp.sum(jnp.where(sels[k], before, 0.0), axis=1, keepdims=True)
        route = jnp.where(lane == k, idx_k, route)
        route = jnp.where(lane == TOP_K + k, pos_k, route)
        gates = jnp.where(lane == k, exps[k] / denom, gates)
    route_ref[...] = route.astype(I32)
    gates_ref[...] = gates
    carry_ref[...] += jnp.sum(onehot, axis=0, keepdims=True)
    cnt_ref[...] = carry_ref[...]


def _route(x2, g, sh, sc, rw_pad, rb_pad, S):
    T, D = x2.shape
    tm = _pick(S, 512, 16)
    tpb = S // tm
    return pl.pallas_call(
        _route_kernel,
        out_shape=(jax.ShapeDtypeStruct((T, D // 2), U32), jax.ShapeDtypeStruct((T, LANES), I32),
                   jax.ShapeDtypeStruct((T, LANES), F32), jax.ShapeDtypeStruct((1, LANES), F32)),
        grid=(T // tm,),
        in_specs=[pl.BlockSpec((tm, D), lambda i: (i, 0)),
                  pl.BlockSpec((1, D), lambda i: (0, 0)),
                  pl.BlockSpec((1, 1, D), lambda i: (i // tpb, 0, 0)),
                  pl.BlockSpec((1, 1, D), lambda i: (i // tpb, 0, 0)),
                  pl.BlockSpec((D, LANES), lambda i: (0, 0)),
                  pl.BlockSpec((1, LANES), lambda i: (0, 0))],
        out_specs=(pl.BlockSpec((tm, D // 2), lambda i: (i, 0)),
                   pl.BlockSpec((tm, LANES), lambda i: (i, 0)),
                   pl.BlockSpec((tm, LANES), lambda i: (i, 0)),
                   pl.BlockSpec((1, LANES), lambda i: (0, 0))),
        scratch_shapes=[pltpu.VMEM((1, LANES), F32)],
        compiler_params=_params(("arbitrary",)),
        name="moe_route",
    )(x2, g, sh, sc, rw_pad, rb_pad)


def _dispatch_kernel(dest_ref, h_ref, xs_ref, sem):
    tm = h_ref.shape[0]
    base = pl.program_id(0) * (tm * TOP_K)

    def row_copy(r, d):
        return pltpu.make_async_copy(h_ref.at[pl.ds(r, 1)], xs_ref.at[pl.ds(d, 1)], sem)

    def issue(r, _):
        for k in range(TOP_K):
            row_copy(r, dest_ref[base + r * TOP_K + k]).start()
        return 0

    def drain(r, _):
        for k in range(TOP_K):
            row_copy(r, dest_ref[base + r * TOP_K + k]).wait()
        return 0

    lax.fori_loop(0, tm, issue, 0)
    lax.fori_loop(0, tm, drain, 0)


def _dispatch(dest, hp, n_rows):
    T, W = hp.shape
    tm = _pick(T, 512, 8)
    return pl.pallas_call(
        _dispatch_kernel,
        out_shape=jax.ShapeDtypeStruct((n_rows, W), U32),
        grid_spec=pltpu.PrefetchScalarGridSpec(
            num_scalar_prefetch=1, grid=(T // tm,),
            in_specs=[pl.BlockSpec((tm, W), lambda i, d: (i, 0))],
            out_specs=pl.BlockSpec(memory_space=pl.ANY),
            scratch_shapes=[pltpu.SemaphoreType.DMA(())]),
        compiler_params=_params(("arbitrary",)),
        name="moe_dispatch",
    )(dest, hp)


def _expert_kernel(be_ref, nu_ref, x_ref, wu_ref, bu_ref, wd_ref, bd_ref, o_ref):
    @pl.when(pl.program_id(0) < nu_ref[0])
    def _():
        lo, hi = _unpack_bf16_pairs(x_ref[...])
        xb = jnp.concatenate([lo.astype(BF16), hi.astype(BF16)], axis=1)
        gu = jnp.dot(xb, wu_ref[0], preferred_element_type=F32) + bu_ref[0]
        F = gu.shape[1] // 2
        g = jnp.minimum(gu[:, :F], SWIGLU_LIMIT)
        lin = jnp.clip(gu[:, F:], -SWIGLU_LIMIT, SWIGLU_LIMIT)
        act = g * _sigmoid(SWIGLU_ALPHA * g) * (lin + 1.0)
        out = jnp.dot(act.astype(BF16), wd_ref[0], preferred_element_type=F32) + bd_ref[0]
        o_ref[...] = _pack_bf16_pairs(out.astype(BF16))


def _experts(block_e, n_used, xs, w_up, b_up, w_down, b_down):
    NP, W = xs.shape
    E, D, F2 = w_up.shape
    bm = MOE_ROWS
    blk = lambda i, be, nu: (jnp.minimum(i, nu[0] - 1), 0)
    return pl.pallas_call(
        _expert_kernel,
        out_shape=jax.ShapeDtypeStruct((NP, W), U32),
        grid_spec=pltpu.PrefetchScalarGridSpec(
            num_scalar_prefetch=2, grid=(NP // bm,),
            in_specs=[pl.BlockSpec((bm, W), blk),
                      pl.BlockSpec((1, D, F2), lambda i, be, nu: (be[i], 0, 0)),
                      pl.BlockSpec((1, 1, F2), lambda i, be, nu: (be[i], 0, 0)),
                      pl.BlockSpec((1, F2 // 2, D), lambda i, be, nu: (be[i], 0, 0)),
                      pl.BlockSpec((1, 1, D), lambda i, be, nu: (be[i], 0, 0))],
            out_specs=pl.BlockSpec((bm, W), blk)),
        compiler_params=_params(("arbitrary",)),
        name="moe_experts",
    )(block_e, n_used, xs, w_up, b_up.reshape(E, 1, F2), w_down, b_down.reshape(E, 1, D))


def _combine_kernel(dest_ref, x_ref, gates_ref, gt_ref, ys_ref, o_ref, buf_ref, sem):
    tm = x_ref.shape[0]
    base = pl.program_id(0) * (tm * TOP_K)

    def row_copy(r, k, d):
        return pltpu.make_async_copy(ys_ref.at[pl.ds(d, 1)], buf_ref.at[k, pl.ds(r, 1)], sem)

    def issue(r, _):
        for k in range(TOP_K):
            row_copy(r, k, dest_ref[base + r * TOP_K + k]).start()
        return 0

    def drain(r, _):
        for k in range(TOP_K):
            row_copy(r, k, dest_ref[base + r * TOP_K + k]).wait()
        return 0

    lax.fori_loop(0, tm, issue, 0)
    lax.fori_loop(0, tm, drain, 0)
    lane = lax.broadcasted_iota(I32, (tm, LANES), 1)
    gates = gates_ref[...]
    W = buf_ref.shape[2]
    y_lo = jnp.zeros((tm, W), F32)
    y_hi = jnp.zeros((tm, W), F32)
    for k in range(TOP_K):
        gk = jnp.sum(jnp.where(lane == k, gates, 0.0), axis=1, keepdims=True)
        lo, hi = _unpack_bf16_pairs(buf_ref[k])
        y_lo = y_lo + gk * lo
        y_hi = y_hi + gk * hi
    gt = gt_ref[0]
    o_ref[:, :W] = x_ref[:, :W] + gt[:, :W] * y_lo
    o_ref[:, W:] = x_ref[:, W:] + gt[:, W:] * y_hi


def _combine(dest, x2, gates, gt, ys, S):
    T, D = x2.shape
    tm = _pick(S, 256, 8)
    tpb = S // tm
    return pl.pallas_call(
        _combine_kernel,
        out_shape=jax.ShapeDtypeStruct((T, D), F32),
        grid_spec=pltpu.PrefetchScalarGridSpec(
            num_scalar_prefetch=1, grid=(T // tm,),
            in_specs=[pl.BlockSpec((tm, D), lambda i, d: (i, 0)),
                      pl.BlockSpec((tm, LANES), lambda i, d: (i, 0)),
                      pl.BlockSpec((1, 1, D), lambda i, d: (i // tpb, 0, 0)),
                      pl.BlockSpec(memory_space=pl.ANY)],
            out_specs=pl.BlockSpec((tm, D), lambda i, d: (i, 0)),
            scratch_shapes=[pltpu.VMEM((TOP_K, tm, D // 2), U32), pltpu.SemaphoreType.DMA(())]),
        compiler_params=_params(("arbitrary",)),
        name="moe_combine",
    )(dest, x2, gates, gt, ys)


def _final_kernel(x_ref, g_ref, o_ref):
    x = x_ref[...]
    var = jnp.mean(x * x, axis=-1, keepdims=True)
    o_ref[...] = x * lax.rsqrt(var + EPS) * g_ref[...]


def _final_norm(x2, g):
    T, D = x2.shape
    tm = _pick(T, 512, 8)
    return pl.pallas_call(
        _final_kernel,
        out_shape=jax.ShapeDtypeStruct((T, D), F32),
        grid=(T // tm,),
        in_specs=[pl.BlockSpec((tm, D), lambda i: (i, 0)), pl.BlockSpec((1, D), lambda i: (0, 0))],
        out_specs=pl.BlockSpec((tm, D), lambda i: (i, 0)),
        compiler_params=_params(("arbitrary",)),
        name="final_norm",
    )(x2, g)


def _routing_tables(route, counts, n_experts, n_blocks):
    bm = MOE_ROWS
    cnt = counts[0, :n_experts].astype(I32)
    padded = (cnt + bm - 1) // bm * bm
    pend = jnp.cumsum(padded)
    pstart = pend - padded
    idx_t = route[:, :TOP_K].T
    base = jnp.zeros_like(idx_t)
    for e in range(n_experts):
        base = jnp.where(idx_t == e, pstart[e], base)
    dest = (base + route[:, TOP_K:2 * TOP_K].T).T.reshape(-1)
    first_row = jnp.arange(n_blocks, dtype=I32) * bm
    block_e = jnp.minimum(jnp.sum((pend[None, :] <= first_row[:, None]).astype(I32), axis=1), n_experts - 1)
    n_used = (pend[-1:] // bm).astype(I32)
    return dest, block_e, n_used


def kernel(x, c, w_ada, b_ada, g_norm_mix, w_in, pool_w, pool_scale, conv_w, conv_b, conv_ln_g, conv_ln_b,
           fgate_b, gmlp_ln_g, gmlp_ws, gmlp_bs, w_gate, b_gate, w_branch, w_o, g_norm_ffn, router_w, router_b,
           w_up, b_up, w_down, b_down, g_final):
    B, S, D = x.shape
    L = w_ada.shape[0]
    T = B * S
    G, GD = pool_w.shape[1], pool_w.shape[2]
    P = G * GD
    C = conv_w.shape[2]
    H = fgate_b.shape[1]
    GW = gmlp_ln_g.shape[1]
    A = (w_in.shape[2] - P - 2 * C - H - 2 * GW) // 3
    dh = A // H
    E = router_w.shape[2]
    off_conv, off_q = P, P + 2 * C
    off_f = off_q + 3 * A
    off_gmlp = off_q + 3 * A
    n_blocks = -(-(T * TOP_K + E * (MOE_ROWS - 1)) // MOE_ROWS)

    c_pad = jnp.pad(c, ((0, (-B) % 8), (0, 0)))
    mod = _ada(c_pad, w_ada, b_ada)
    x2 = x.reshape(T, D)

    for l in range(L):
        sh1, sc1, gt1, sh2, sc2, gt2 = (mod[l, :B, i * D:(i + 1) * D].reshape(B, 1, D) for i in range(6))
        w_main = jnp.concatenate([w_in[l][:, :off_f], w_in[l][:, off_f + H:]], axis=1).astype(BF16)
        w_f = jnp.pad(w_in[l][:, off_f:off_f + H], ((0, 0), (0, LANES - H))).astype(BF16)
        h, z, f = _in_proj(x2, g_norm_mix[l].reshape(1, D), sh1, sc1, w_main, w_f, S)
        z3 = z.reshape(B, S, -1)

        y_a = _pool(z3, pool_w[l].astype(BF16), pool_scale[l].reshape(1, P))
        y_b = _conv(z3, off_conv, conv_w[l], conv_b[l].reshape(1, C), conv_ln_g[l].reshape(1, C),
                    conv_ln_b[l].reshape(1, C))
        fb_pad = jnp.pad(fgate_b[l], (0, LANES - H)).reshape(1, LANES)
        fcol = _cumf(f.reshape(B, S, LANES), fb_pad)
        y_c = _attn(z3, off_q, H, dh, fcol)
        bs_b = jnp.broadcast_to(gmlp_bs[l][:, :, None], (G, GMLP_CHUNK, GW // G))
        y_d = _gmlp(z, off_gmlp, GW, gmlp_ln_g[l].reshape(1, GW), gmlp_ws[l], bs_b)

        wb = w_branch[l].astype(BF16)
        wbr = (wb[:P], wb[P:P + C], wb[P + C:P + C + A], wb[P + C + A:])
        x2 = _merge(h, y_a.reshape(T, P), y_b.reshape(T, C), y_c.reshape(T, A), y_d, x2, gt1,
                    w_gate[l].astype(BF16), b_gate[l].reshape(4, 1, D), wbr, w_o[l].astype(BF16), S)

        rw_pad = jnp.pad(router_w[l], ((0, 0), (0, LANES - E))).astype(BF16)
        rb_pad = jnp.pad(router_b[l], (0, LANES - E), constant_values=-1e30).reshape(1, LANES)
        hp, route, gates, counts = _route(x2, g_norm_ffn[l].reshape(1, D), sh2, sc2, rw_pad, rb_pad, S)
        dest, block_e, n_used = _routing_tables(route, counts, E, n_blocks)
        xs = _dispatch(dest, hp, n_blocks * MOE_ROWS)
        ys = _experts(block_e, n_used, xs, w_up[l].astype(BF16), b_up[l], w_down[l].astype(BF16), b_down[l])
        x2 = _combine(dest, x2, gates, gt2, ys, S)

    return _final_norm(x2, g_final.reshape(1, D)).reshape(B, S, D)
```

```python
import functools
import math

import jax
import jax.numpy as jnp
from jax import lax
from jax.experimental import pallas as pl
from jax.experimental.pallas import tpu as pltpu

F32 = jnp.float32
BF16 = jnp.bfloat16
U32 = jnp.uint32
I32 = jnp.int32

EPS = 1e-5
POOL_WINDOWS = (2, 4, 8, 16)
CONV_KERNEL = 31
CONV_HALO = 32
GMLP_CHUNK = 128
TOP_K = 4
SWIGLU_LIMIT = 7.0
SWIGLU_ALPHA = 1.702
LANES = 128
SUBLANES = 8
MOE_ROWS = 256
VMEM_LIMIT = 56 * 1024 * 1024
HIGHEST = lax.Precision.HIGHEST


def _pick(n, target, mult):
    if n <= target:
        return n
    best = None
    for d in range(mult, target + 1, mult):
        if n % d == 0:
            best = d
    assert best is not None, (n, target, mult)
    return best


def _params(sem):
    return pltpu.CompilerParams(dimension_semantics=sem, vmem_limit_bytes=VMEM_LIMIT)


def _sigmoid(x):
    return 1.0 / (1.0 + jnp.exp(-x))


def _ada_kernel(c_ref, w_ref, b_ref, o_ref):
    c = c_ref[...]
    ca = c * _sigmoid(c)
    o_ref[0] = jnp.dot(ca, w_ref[0], preferred_element_type=F32, precision=HIGHEST) + b_ref[0]


def _ada(c_pad, w_ada, b_ada):
    L, D, N = w_ada.shape
    R = c_pad.shape[0]
    tn = _pick(N, 1024, LANES)
    return pl.pallas_call(
        _ada_kernel,
        out_shape=jax.ShapeDtypeStruct((L, R, N), F32),
        grid=(L, N // tn),
        in_specs=[pl.BlockSpec((R, D), lambda l, j: (0, 0)),
                  pl.BlockSpec((1, D, tn), lambda l, j: (l, 0, j)),
                  pl.BlockSpec((1, 1, tn), lambda l, j: (l, 0, j))],
        out_specs=pl.BlockSpec((1, R, tn), lambda l, j: (l, 0, j)),
        compiler_params=_params(("arbitrary", "arbitrary")),
        name="ada_mod",
    )(c_pad, w_ada, b_ada.reshape(L, 1, N))


def _norm_mod(x, g, sc, sh):
    var = jnp.mean(x * x, axis=-1, keepdims=True)
    y = x * lax.rsqrt(var + EPS) * g
    return y * (1.0 + sc) + sh


def _in_kernel(x_ref, g_ref, sh_ref, sc_ref, w_ref, wf_ref, h_ref, z_ref, f_ref):
    @pl.when(pl.program_id(1) == 0)
    def _():
        hb = _norm_mod(x_ref[...], g_ref[...], sc_ref[0], sh_ref[0]).astype(BF16)
        h_ref[...] = hb
        f_ref[...] = jnp.dot(hb, wf_ref[...], preferred_element_type=F32)

    z_ref[...] = jnp.dot(h_ref[...], w_ref[...], preferred_element_type=F32).astype(BF16)


def _in_proj(x2, g, sh, sc, w_main, w_f, l, S):
    T, D = x2.shape
    N = w_main.shape[2]
    tm = _pick(S, 512, 16)
    tn = _pick(N, 1408, LANES)
    tpb = S // tm
    return pl.pallas_call(
        _in_kernel,
        out_shape=(jax.ShapeDtypeStruct((T, D), BF16), jax.ShapeDtypeStruct((T, N), BF16),
                   jax.ShapeDtypeStruct((T, LANES), F32)),
        grid=(T // tm, N // tn),
        in_specs=[pl.BlockSpec((tm, D), lambda i, j: (i, 0)),
                  pl.BlockSpec((1, D), lambda i, j: (0, 0)),
                  pl.BlockSpec((1, 1, D), lambda i, j: (i // tpb, 0, 0)),
                  pl.BlockSpec((1, 1, D), lambda i, j: (i // tpb, 0, 0)),
                  pl.BlockSpec((None, D, tn), lambda i, j: (l, 0, j)),
                  pl.BlockSpec((None, D, LANES), lambda i, j: (l, 0, 0))],
        out_specs=(pl.BlockSpec((tm, D), lambda i, j: (i, 0)),
                   pl.BlockSpec((tm, tn), lambda i, j: (i, j)),
                   pl.BlockSpec((tm, LANES), lambda i, j: (i, 0))),
        compiler_params=_params(("arbitrary", "arbitrary")),
        name="in_proj",
    )(x2, g, sh, sc, w_main, w_f)


def _pool_kernel(z_ref, w_ref, s_ref, o_ref):
    S = z_ref.shape[1]
    G, GD, _ = w_ref.shape
    row = lax.broadcasted_iota(I32, (S, GD), 0)
    for g in range(G):
        wnd = POOL_WINDOWS[g]
        a = z_ref[0, :, g * GD:(g + 1) * GD].astype(F32)
        s = a
        k = 1
        while k < wnd:
            s = s + jnp.where(row >= k, pltpu.roll(s, k, axis=0), 0.0)
            k *= 2
        cnt = jnp.minimum(row + 1, wnd).astype(F32)
        mixed = s / cnt - a
        y = jnp.dot(mixed.astype(BF16), w_ref[g], preferred_element_type=F32)
        o_ref[0, :, g * GD:(g + 1) * GD] = (y * s_ref[:, g * GD:(g + 1) * GD]).astype(BF16)


def _pool(z3, pool_w, pool_scale):
    B, S, _ = z3.shape
    G, GD, _ = pool_w.shape
    P = G * GD
    return pl.pallas_call(
        _pool_kernel,
        out_shape=jax.ShapeDtypeStruct((B, S, P), BF16),
        grid=(B,),
        in_specs=[pl.BlockSpec((1, S, P), lambda b: (b, 0, 0)),
                  pl.BlockSpec((G, GD, GD), lambda b: (0, 0, 0)),
                  pl.BlockSpec((1, P), lambda b: (0, 0))],
        out_specs=pl.BlockSpec((1, S, P), lambda b: (b, 0, 0)),
        compiler_params=_params(("arbitrary",)),
        name="pool_mixer",
    )(z3, pool_w, pool_scale)


CONV_ROWS = 64


def _conv_kernel(za_ref, zb_ref, ha_ref, hb_ref, w_ref, b_ref, g_ref, beta_ref, o_ref, ext_ref, sh_ref):
    ts = za_ref.shape[1]
    C = za_ref.shape[2]
    halo = ha_ref[0].astype(F32) * _sigmoid(hb_ref[0].astype(F32))
    ext_ref[0:CONV_HALO, :] = jnp.where(pl.program_id(1) > 0, halo, 0.0)
    ext_ref[CONV_HALO:CONV_HALO + ts, :] = za_ref[0].astype(F32) * _sigmoid(zb_ref[0].astype(F32))
    n_sh = CONV_HALO + ts - SUBLANES
    sh_ref[0] = ext_ref[...]
    for b in range(1, SUBLANES):
        sh_ref[b, 0:n_sh, :] = ext_ref[b:b + n_sh, :]
    first = CONV_HALO - (CONV_KERNEL - 1)
    R = min(CONV_ROWS, ts)
    for r0 in range(0, ts, R):
        acc = jnp.broadcast_to(b_ref[...], (R, C))
        for j in range(CONV_KERNEL):
            a, b = divmod(first + j, SUBLANES)
            acc = acc + w_ref[j:j + 1, :] * sh_ref[b, r0 + a * SUBLANES:r0 + a * SUBLANES + R, :]
        mu = jnp.mean(acc, axis=-1, keepdims=True)
        xc = acc - mu
        var = jnp.mean(xc * xc, axis=-1, keepdims=True)
        y = xc * lax.rsqrt(var + EPS) * g_ref[...] + beta_ref[...]
        o_ref[0, r0:r0 + R, :] = (y * _sigmoid(y)).astype(BF16)


def _conv(z3, off_a, conv_w, conv_b, ln_g, ln_b):
    B, S, _ = z3.shape
    K, C = conv_w.shape
    ts = _pick(S, 512, CONV_HALO)
    ia, ib = off_a // C, off_a // C + 1
    hpt = ts // CONV_HALO
    assert off_a % C == 0
    halo_map = lambda col: (lambda b, i: (b, jnp.maximum(i * hpt - 1, 0), col))
    return pl.pallas_call(
        _conv_kernel,
        out_shape=jax.ShapeDtypeStruct((B, S, C), BF16),
        grid=(B, S // ts),
        in_specs=[pl.BlockSpec((1, ts, C), lambda b, i: (b, i, ia)),
                  pl.BlockSpec((1, ts, C), lambda b, i: (b, i, ib)),
                  pl.BlockSpec((1, CONV_HALO, C), halo_map(ia)),
                  pl.BlockSpec((1, CONV_HALO, C), halo_map(ib)),
                  pl.BlockSpec((K, C), lambda b, i: (0, 0)),
                  pl.BlockSpec((1, C), lambda b, i: (0, 0)),
                  pl.BlockSpec((1, C), lambda b, i: (0, 0)),
                  pl.BlockSpec((1, C), lambda b, i: (0, 0))],
        out_specs=pl.BlockSpec((1, ts, C), lambda b, i: (b, i, 0)),
        scratch_shapes=[pltpu.VMEM((CONV_HALO + ts, C), F32),
                        pltpu.VMEM((SUBLANES, CONV_HALO + ts, C), F32)],
        compiler_params=_params(("arbitrary", "arbitrary")),
        name="conv_mixer",
    )(z3, z3, z3, z3, conv_w, conv_b, ln_g, ln_b)


CUM_CHUNK = 256


def _cumf_kernel(f_ref, b_ref, fc_ref):
    S = f_ref.shape[1]
    C = min(CUM_CHUNK, S)
    x = f_ref[0] + b_ref[...]
    ls = -(jnp.maximum(-x, 0.0) + jnp.log1p(jnp.exp(-jnp.abs(x))))
    tri = (lax.broadcasted_iota(I32, (C, C), 0) >= lax.broadcasted_iota(I32, (C, C), 1)).astype(F32)
    carry = jnp.zeros((1, LANES), F32)
    for c in range(S // C):
        cs = jnp.dot(tri, ls[c * C:(c + 1) * C], preferred_element_type=F32, precision=HIGHEST) + carry
        fc_ref[0, c * C:(c + 1) * C, :] = cs
        carry = cs[C - 1:C, :]


def _cumf(f3, fb_pad):
    B, S, _ = f3.shape
    return pl.pallas_call(
        _cumf_kernel,
        out_shape=jax.ShapeDtypeStruct((B, S, LANES), F32),
        grid=(B,),
        in_specs=[pl.BlockSpec((1, S, LANES), lambda b: (b, 0, 0)),
                  pl.BlockSpec((1, LANES), lambda b: (0, 0))],
        out_specs=pl.BlockSpec((1, S, LANES), lambda b: (b, 0, 0)),
        compiler_params=_params(("arbitrary",)),
        name="forget_cumsum",
    )(f3, fb_pad)


ATTN_STRIP = 32
LOG2E = 1.4426950408889634


def _split3(x):
    hi = x.astype(BF16).astype(F32)
    mid = (x - hi).astype(BF16).astype(F32)
    return hi, mid, x - hi - mid


def _attn_kernel(q_ref, k_ref, v_ref, fc_ref, o_ref, ka_ref, fh_ref, qa_ref, s_ref, p_ref, m_ref, l_ref, acc_ref,
                 *, scale):
    h = pl.program_id(1)
    i = pl.program_id(2)
    tq = q_ref.shape[1]
    S = k_ref.shape[1]
    nl = tq // LANES
    R = ATTN_STRIP
    q_start = pl.multiple_of(i * tq, tq)

    @pl.when(i == 0)
    def _():
        lane = lax.broadcasted_iota(I32, (S, LANES), 1)
        fh = jnp.sum(jnp.where(lane == h, fc_ref[0], 0.0), axis=1, keepdims=True) * LOG2E
        fh_ref[...] = jnp.broadcast_to(fh, (S, LANES))
        hi, mid, lo = _split3(fh)
        aug = jnp.where(lane == 0, hi, jnp.where(lane == 1, mid, jnp.where(lane == 2, lo,
                                                                           jnp.where(lane < 6, 1.0, 0.0))))
        ka_ref[:, 0:LANES] = k_ref[0]
        ka_ref[:, LANES:2 * LANES] = aug.astype(BF16)

    hi0, mid0, lo0 = _split3(fh_ref[pl.ds(q_start, SUBLANES), :][0:1, :])
    lane1 = lax.broadcasted_iota(I32, (1, LANES), 1)
    qrow = jnp.where(lane1 < 3, -1.0, jnp.where(lane1 == 3, hi0, jnp.where(lane1 == 4, mid0,
                                                                           jnp.where(lane1 == 5, lo0, 0.0))))
    qa_ref[:, 0:LANES] = (q_ref[0].astype(F32) * (scale * LOG2E)).astype(BF16)
    qa_ref[:, LANES:2 * LANES] = jnp.broadcast_to(qrow, (tq, LANES)).astype(BF16)
    m_ref[...] = jnp.full(m_ref.shape, -jnp.inf, F32)
    l_ref[...] = jnp.zeros(l_ref.shape, F32)
    acc_ref[...] = jnp.zeros(acc_ref.shape, F32)

    def logits(j, slot):
        start = pl.multiple_of(j * tq, tq)
        s_ref[slot] = lax.dot_general(qa_ref[...], ka_ref[pl.ds(start, tq), :], (((1,), (1,)), ((), ())),
                                      preferred_element_type=F32)

    def softmax_pv(j, slot, masked):
        start = pl.multiple_of(j * tq, tq)

        def scaled(r0, c):
            t = s_ref[slot, r0:r0 + R, c * LANES:(c + 1) * LANES]
            if masked:
                row = lax.broadcasted_iota(I32, (R, LANES), 0) + r0
                col = lax.broadcasted_iota(I32, (R, LANES), 1) + c * LANES
                t = jnp.where(col <= row, t, -jnp.inf)
            return t

        def live(r0):
            return [c for c in range(nl) if not masked or c * LANES < r0 + R]

        for r0 in range(0, tq, R):
            mx = scaled(r0, 0)
            for c in live(r0)[1:]:
                mx = jnp.maximum(mx, scaled(r0, c))
            m_old = m_ref[r0:r0 + R, :]
            m_new = jnp.maximum(m_old, jnp.max(mx, axis=1, keepdims=True))
            alpha = jnp.exp2(m_old - m_new)
            m_ref[r0:r0 + R, :] = m_new
            l_ref[r0:r0 + R, :] = alpha * l_ref[r0:r0 + R, :]
            acc_ref[r0:r0 + R, :] = alpha * acc_ref[r0:r0 + R, :]
        for r0 in range(0, tq, R):
            m_new = m_ref[r0:r0 + R, :]
            rs = None
            for c in range(nl):
                if c in live(r0):
                    p = jnp.exp2(scaled(r0, c) - m_new)
                    rs = p if rs is None else rs + p
                else:
                    p = jnp.zeros((R, LANES), F32)
                p_ref[r0:r0 + R, c * LANES:(c + 1) * LANES] = p.astype(BF16)
            l_ref[r0:r0 + R, :] += rs
        acc_ref[...] += jnp.dot(p_ref[...], v_ref[0, pl.ds(start, tq), :], preferred_element_type=F32)

    logits(0, 0)

    def pair(k, carry):
        logits(2 * k + 1, 1)
        softmax_pv(2 * k, 0, False)
        logits(2 * k + 2, 0)
        softmax_pv(2 * k + 1, 1, False)
        return carry

    lax.fori_loop(0, i // 2, pair, 0)

    @pl.when(i % 2 == 1)
    def _():
        logits(i, 1)
        softmax_pv(i - 1, 0, False)
        softmax_pv(i, 1, True)

    @pl.when(i % 2 == 0)
    def _():
        softmax_pv(i, 0, True)

    o_ref[0] = (acc_ref[...] / jnp.sum(l_ref[...], axis=1, keepdims=True)).astype(BF16)


def _attn(z3, off_q, H, dh, fcol):
    B, S, _ = z3.shape
    tq = _pick(S, 512, LANES)
    iq, ik, iv = off_q // dh, off_q // dh + H, off_q // dh + 2 * H
    assert off_q % dh == 0 and dh == LANES and tq % ATTN_STRIP == 0
    return pl.pallas_call(
        functools.partial(_attn_kernel, scale=1.0 / math.sqrt(dh)),
        out_shape=jax.ShapeDtypeStruct((B, S, H * dh), BF16),
        grid=(B, H, S // tq),
        in_specs=[pl.BlockSpec((1, tq, dh), lambda b, h, i: (b, i, iq + h)),
                  pl.BlockSpec((1, S, dh), lambda b, h, i: (b, 0, ik + h)),
                  pl.BlockSpec((1, S, dh), lambda b, h, i: (b, 0, iv + h)),
                  pl.BlockSpec((1, S, LANES), lambda b, h, i: (b, 0, 0))],
        out_specs=pl.BlockSpec((1, tq, dh), lambda b, h, i: (b, i, h)),
        scratch_shapes=[pltpu.VMEM((S, 2 * LANES), BF16), pltpu.VMEM((S, LANES), F32),
                        pltpu.VMEM((tq, 2 * LANES), BF16), pltpu.VMEM((2, tq, tq), F32),
                        pltpu.VMEM((tq, tq), BF16), pltpu.VMEM((tq, LANES), F32), pltpu.VMEM((tq, LANES), F32),
                        pltpu.VMEM((tq, dh), F32)],
        compiler_params=_params(("arbitrary", "arbitrary", "arbitrary")),
        name="forget_attn",
    )(z3, z3, z3, fcol)


def _gelu_tanh(x):
    return 0.5 * x * (1.0 + jnp.tanh(math.sqrt(2.0 / math.pi) * (x + 0.044715 * (x * x * x))))


def _gmlp_kernel(u_ref, v_ref, g_ref, ws_ref, bs_ref, o_ref):
    ts, GW = u_ref.shape
    G = ws_ref.shape[0]
    GD = GW // G
    ck = GMLP_CHUNK
    u = _gelu_tanh(u_ref[...].astype(F32))
    v = _gelu_tanh(v_ref[...].astype(F32))
    var = jnp.mean(v * v, axis=-1, keepdims=True)
    vb = (v * lax.rsqrt(var + EPS) * g_ref[...]).astype(BF16)
    causal = lax.broadcasted_iota(I32, (ck, ck), 0) >= lax.broadcasted_iota(I32, (ck, ck), 1)
    for g in range(G):
        wm = jnp.where(causal, ws_ref[g], 0.0).astype(BF16)
        for c in range(ts // ck):
            sv = jnp.dot(wm, vb[c * ck:(c + 1) * ck, g * GD:(g + 1) * GD], preferred_element_type=F32)
            sv = sv + bs_ref[g]
            o_ref[c * ck:(c + 1) * ck, g * GD:(g + 1) * GD] = (
                u[c * ck:(c + 1) * ck, g * GD:(g + 1) * GD] * sv).astype(BF16)


def _gmlp(z2, off_u, GW, ln_g, ws, bs_b):
    T, _ = z2.shape
    G = ws.shape[0]
    ts = _pick(T, 512, GMLP_CHUNK)
    iu = off_u // GW
    assert off_u % GW == 0
    return pl.pallas_call(
        _gmlp_kernel,
        out_shape=jax.ShapeDtypeStruct((T, GW), BF16),
        grid=(T // ts,),
        in_specs=[pl.BlockSpec((ts, GW), lambda i: (i, iu)),
                  pl.BlockSpec((ts, GW), lambda i: (i, iu + 1)),
                  pl.BlockSpec((1, GW), lambda i: (0, 0)),
                  pl.BlockSpec((G, GMLP_CHUNK, GMLP_CHUNK), lambda i: (0, 0, 0)),
                  pl.BlockSpec((G, GMLP_CHUNK, GW // G), lambda i: (0, 0, 0))],
        out_specs=pl.BlockSpec((ts, GW), lambda i: (i, 0)),
        compiler_params=_params(("arbitrary",)),
        name="gmlp_mixer",
    )(z2, z2, ln_g, ws, bs_b)


def _merge_kernel(h_ref, ya_ref, yb_ref, yc_ref, yd_ref, wg0_ref, wg1_ref, wg2_ref, wg3_ref, bg_ref,
                  wa_ref, wb_ref, wc_ref, wd_ref, o_ref):
    h = h_ref[...]
    merged = None
    for br, (y_ref, wg_ref, wbr_ref) in enumerate(((ya_ref, wg0_ref, wa_ref), (yb_ref, wg1_ref, wb_ref),
                                                   (yc_ref, wg2_ref, wc_ref), (yd_ref, wg3_ref, wd_ref))):
        gate = _sigmoid(jnp.dot(h, wg_ref[...], preferred_element_type=F32) + bg_ref[br])
        term = gate * jnp.dot(y_ref[...], wbr_ref[...], preferred_element_type=F32)
        merged = term if merged is None else merged + term
    o_ref[...] = merged.astype(BF16)


def _merge(h, ys, w_gate, b_gate4, w_branch, l, S):
    T, D = h.shape
    tm = _pick(S, 512, 16)
    tn = _pick(D, 512, LANES)
    nb = D // tn
    row = lambda width: pl.BlockSpec((tm, width), lambda i, n: (i, 0))
    wg = lambda br: pl.BlockSpec((None, D, tn), lambda i, n: (l, 0, br * nb + n))
    wb_specs, off = [], 0
    for y in ys:
        rows = y.shape[1]
        assert off % rows == 0
        wb_specs.append(pl.BlockSpec((None, rows, tn), lambda i, n, rb=off // rows: (l, rb, n)))
        off += rows
    return pl.pallas_call(
        _merge_kernel,
        out_shape=jax.ShapeDtypeStruct((T, D), BF16),
        grid=(T // tm, nb),
        in_specs=[row(D)] + [row(y.shape[1]) for y in ys] + [wg(0), wg(1), wg(2), wg(3),
                  pl.BlockSpec((4, 1, tn), lambda i, n: (0, 0, n))] + wb_specs,
        out_specs=pl.BlockSpec((tm, tn), lambda i, n: (i, n)),
        compiler_params=_params(("arbitrary", "arbitrary")),
        name="gated_merge",
    )(h, *ys, w_gate, w_gate, w_gate, w_gate, b_gate4, w_branch, w_branch, w_branch, w_branch)


def _out_kernel(m_ref, x_ref, gt_ref, wo_ref, o_ref):
    b = pl.program_id(0) // (pl.num_programs(0) // gt_ref.shape[0])
    o_ref[...] = x_ref[...] + gt_ref[b] * jnp.dot(m_ref[...], wo_ref[...], preferred_element_type=F32)


def _out_proj(merged, x2, gt, w_o, l, S):
    T, D = x2.shape
    tm = _pick(S, 512, 16)
    B = gt.shape[0]
    return pl.pallas_call(
        _out_kernel,
        out_shape=jax.ShapeDtypeStruct((T, D), F32),
        grid=(T // tm,),
        in_specs=[pl.BlockSpec((tm, D), lambda i: (i, 0)),
                  pl.BlockSpec((tm, D), lambda i: (i, 0)),
                  pl.BlockSpec((B, 1, D), lambda i: (0, 0, 0)),
                  pl.BlockSpec((None, D, D), lambda i: (l, 0, 0))],
        out_specs=pl.BlockSpec((tm, D), lambda i: (i, 0)),
        compiler_params=_params(("arbitrary",)),
        name="out_proj",
    )(merged, x2, gt, w_o)


def _pack_bf16_pairs(hb):
    W = hb.shape[1] // 2
    lo = pltpu.bitcast(hb[:, :W].astype(F32), U32)
    hi = pltpu.bitcast(hb[:, W:].astype(F32), U32)
    return (hi & jnp.uint32(0xFFFF0000)) | (lo >> 16)


def _unpack_bf16_pairs(p):
    lo = pltpu.bitcast(p << 16, F32)
    hi = pltpu.bitcast(p & jnp.uint32(0xFFFF0000), F32)
    return lo, hi


def _route_kernel(x_ref, g_ref, sh_ref, sc_ref, rw_ref, rb_ref, hp_ref, route_ref, gates_ref, cnt_ref, carry_ref):
    i = pl.program_id(0)
    tm = x_ref.shape[0]

    @pl.when(i == 0)
    def _():
        carry_ref[...] = jnp.zeros_like(carry_ref)

    hb = _norm_mod(x_ref[...], g_ref[...], sc_ref[0], sh_ref[0]).astype(BF16)
    hp_ref[...] = _pack_bf16_pairs(hb)
    logits = jnp.dot(hb, rw_ref[...], preferred_element_type=F32) + rb_ref[...]
    lane = lax.broadcasted_iota(I32, (tm, LANES), 1)
    lanef = lane.astype(F32)
    vals, sels = [], []
    for _ in range(TOP_K):
        m = jnp.max(logits, axis=1, keepdims=True)
        idx = jnp.min(jnp.where(logits == m, lanef, float(LANES)), axis=1, keepdims=True)
        sel = lanef == idx
        logits = jnp.where(sel, -jnp.inf, logits)
        vals.append(m)
        sels.append(sel)
    exps = [jnp.exp(v - vals[0]) for v in vals]
    denom = exps[0] + exps[1] + exps[2] + exps[3]
    onehot = jnp.zeros((tm, LANES), F32)
    for sel in sels:
        onehot = jnp.where(sel, 1.0, onehot)
    strict = (lax.broadcasted_iota(I32, (tm, tm), 0) > lax.broadcasted_iota(I32, (tm, tm), 1)).astype(BF16)
    before = jnp.dot(strict, onehot.astype(BF16), preferred_element_type=F32) + carry_ref[...]
    route = jnp.zeros((tm, LANES), F32)
    gates = jnp.zeros((tm, LANES), F32)
    for k in range(TOP_K):
        idx_k = jnp.sum(jnp.where(sels[k], lanef, 0.0), axis=1, keepdims=True)
        pos_k = jnp.sum(jnp.where(sels[k], before, 0.0), axis=1, keepdims=True)
        route = jnp.where(lane == k, idx_k, route)
        route = jnp.where(lane == TOP_K + k, pos_k, route)
        gates = jnp.where(lane == k, exps[k] / denom, gates)
    route_ref[...] = route.astype(I32)
    gates_ref[...] = gates
    carry_ref[...] += jnp.sum(onehot, axis=0, keepdims=True)
    cnt_ref[...] = carry_ref[...]


def _route(x2, g, sh, sc, rw_pad, rb_pad, S):
    T, D = x2.shape
    tm = _pick(S, 512, 16)
    tpb = S // tm
    return pl.pallas_call(
        _route_kernel,
        out_shape=(jax.ShapeDtypeStruct((T, D // 2), U32), jax.ShapeDtypeStruct((T, LANES), I32),
                   jax.ShapeDtypeStruct((T, LANES), F32), jax.ShapeDtypeStruct((1, LANES), F32)),
        grid=(T // tm,),
        in_specs=[pl.BlockSpec((tm, D), lambda i: (i, 0)),
                  pl.BlockSpec((1, D), lambda i: (0, 0)),
                  pl.BlockSpec((1, 1, D), lambda i: (i // tpb, 0, 0)),
                  pl.BlockSpec((1, 1, D), lambda i: (i // tpb, 0, 0)),
                  pl.BlockSpec((D, LANES), lambda i: (0, 0)),
                  pl.BlockSpec((1, LANES), lambda i: (0, 0))],
        out_specs=(pl.BlockSpec((tm, D // 2), lambda i: (i, 0)),
                   pl.BlockSpec((tm, LANES), lambda i: (i, 0)),
                   pl.BlockSpec((tm, LANES), lambda i: (i, 0)),
                   pl.BlockSpec((1, LANES), lambda i: (0, 0))),
        scratch_shapes=[pltpu.VMEM((1, LANES), F32)],
        compiler_params=_params(("arbitrary",)),
        name="moe_route",
    )(x2, g, sh, sc, rw_pad, rb_pad)


def _dispatch_kernel(dest_ref, h_ref, xs_ref, sem):
    tm = h_ref.shape[0]
    base = pl.program_id(0) * (tm * TOP_K)

    def row_copy(r, d):
        return pltpu.make_async_copy(h_ref.at[pl.ds(r, 1)], xs_ref.at[pl.ds(d, 1)], sem)

    def issue(r, _):
        for k in range(TOP_K):
            row_copy(r, dest_ref[base + r * TOP_K + k]).start()
        return 0

    def drain(r, _):
        for k in range(TOP_K):
            row_copy(r, dest_ref[base + r * TOP_K + k]).wait()
        return 0

    lax.fori_loop(0, tm, issue, 0)
    lax.fori_loop(0, tm, drain, 0)


def _dispatch(dest, hp, n_rows):
    T, W = hp.shape
    tm = _pick(T, 512, 8)
    return pl.pallas_call(
        _dispatch_kernel,
        out_shape=jax.ShapeDtypeStruct((n_rows, W), U32),
        grid_spec=pltpu.PrefetchScalarGridSpec(
            num_scalar_prefetch=1, grid=(T // tm,),
            in_specs=[pl.BlockSpec((tm, W), lambda i, d: (i, 0))],
            out_specs=pl.BlockSpec(memory_space=pl.ANY),
            scratch_shapes=[pltpu.SemaphoreType.DMA(())]),
        compiler_params=_params(("arbitrary",)),
        name="moe_dispatch",
    )(dest, hp)


CAST_ROWS = 256


def _expert_kernel(be_ref, nu_ref, first_ref, nxt_ref, x_ref, wu_hbm, bu_ref, wd_hbm, bd_ref, o_ref,
                   su_ref, sd_ref, wub_ref, wdb_ref, sem, *, layer):
    b = pl.program_id(0)

    def fetch(e):
        return (pltpu.make_async_copy(wu_hbm.at[layer, e], su_ref, sem.at[0]),
                pltpu.make_async_copy(wd_hbm.at[layer, e], sd_ref, sem.at[1]))

    @pl.when(b == 0)
    def _():
        for cp in fetch(be_ref[0]):
            cp.start()

    @pl.when((b < nu_ref[0]) & (first_ref[b] == 1))
    def _():
        for cp in fetch(be_ref[b]):
            cp.wait()

        def cast(src, dst):
            chunk = _pick(src.shape[0], CAST_ROWS, 16)

            def step(r, _):
                rows = pl.ds(pl.multiple_of(r * chunk, chunk), chunk)
                dst[rows, :] = src[rows, :].astype(BF16)
                return 0
            lax.fori_loop(0, src.shape[0] // chunk, step, 0)

        cast(su_ref, wub_ref)
        cast(sd_ref, wdb_ref)

        @pl.when(nxt_ref[b] >= 0)
        def _():
            for cp in fetch(nxt_ref[b]):
                cp.start()

    @pl.when(b < nu_ref[0])
    def _():
        lo, hi = _unpack_bf16_pairs(x_ref[...])
        xb = jnp.concatenate([lo.astype(BF16), hi.astype(BF16)], axis=1)
        gu = jnp.dot(xb, wub_ref[...], preferred_element_type=F32) + bu_ref[0]
        F = gu.shape[1] // 2
        g = jnp.minimum(gu[:, :F], SWIGLU_LIMIT)
        lin = jnp.clip(gu[:, F:], -SWIGLU_LIMIT, SWIGLU_LIMIT)
        act = g * _sigmoid(SWIGLU_ALPHA * g) * (lin + 1.0)
        out = jnp.dot(act.astype(BF16), wdb_ref[...], preferred_element_type=F32) + bd_ref[0]
        o_ref[...] = _pack_bf16_pairs(out.astype(BF16))


def _experts(block_e, n_used, first, nxt, xs, w_up, b_up, w_down, b_down, l):
    NP, W = xs.shape
    _, E, D, F2 = w_up.shape
    bm = MOE_ROWS
    blk = lambda i, be, nu, fi, nx: (jnp.minimum(i, nu[0] - 1), 0)
    return pl.pallas_call(
        functools.partial(_expert_kernel, layer=l),
        out_shape=jax.ShapeDtypeStruct((NP, W), U32),
        grid_spec=pltpu.PrefetchScalarGridSpec(
            num_scalar_prefetch=4, grid=(NP // bm,),
            in_specs=[pl.BlockSpec((bm, W), blk),
                      pl.BlockSpec(memory_space=pl.ANY),
                      pl.BlockSpec((1, 1, F2), lambda i, be, nu, fi, nx: (l * E + be[i], 0, 0)),
                      pl.BlockSpec(memory_space=pl.ANY),
                      pl.BlockSpec((1, 1, D), lambda i, be, nu, fi, nx: (l * E + be[i], 0, 0))],
            out_specs=pl.BlockSpec((bm, W), blk),
            scratch_shapes=[pltpu.VMEM((D, F2), F32), pltpu.VMEM((F2 // 2, D), F32),
                            pltpu.VMEM((D, F2), BF16), pltpu.VMEM((F2 // 2, D), BF16),
                            pltpu.SemaphoreType.DMA((2,))]),
        compiler_params=_params(("arbitrary",)),
        name="moe_experts",
    )(block_e, n_used, first, nxt, xs, w_up, b_up, w_down, b_down)


def _combine_kernel(dest_ref, x_ref, gates_ref, gt_ref, ys_ref, o_ref, buf_ref, sem):
    tm = x_ref.shape[0]
    base = pl.program_id(0) * (tm * TOP_K)

    def row_copy(r, k, d):
        return pltpu.make_async_copy(ys_ref.at[pl.ds(d, 1)], buf_ref.at[k, pl.ds(r, 1)], sem)

    def issue(r, _):
        for k in range(TOP_K):
            row_copy(r, k, dest_ref[base + r * TOP_K + k]).start()
        return 0

    def drain(r, _):
        for k in range(TOP_K):
            row_copy(r, k, dest_ref[base + r * TOP_K + k]).wait()
        return 0

    lax.fori_loop(0, tm, issue, 0)
    lax.fori_loop(0, tm, drain, 0)
    lane = lax.broadcasted_iota(I32, (tm, LANES), 1)
    gates = gates_ref[...]
    W = buf_ref.shape[2]
    y_lo = jnp.zeros((tm, W), F32)
    y_hi = jnp.zeros((tm, W), F32)
    for k in range(TOP_K):
        gk = jnp.sum(jnp.where(lane == k, gates, 0.0), axis=1, keepdims=True)
        lo, hi = _unpack_bf16_pairs(buf_ref[k])
        y_lo = y_lo + gk * lo
        y_hi = y_hi + gk * hi
    gt = gt_ref[0]
    o_ref[:, :W] = x_ref[:, :W] + gt[:, :W] * y_lo
    o_ref[:, W:] = x_ref[:, W:] + gt[:, W:] * y_hi


def _combine(dest, x2, gates, gt, ys, S):
    T, D = x2.shape
    tm = _pick(S, 256, 8)
    tpb = S // tm
    return pl.pallas_call(
        _combine_kernel,
        out_shape=jax.ShapeDtypeStruct((T, D), F32),
        grid_spec=pltpu.PrefetchScalarGridSpec(
            num_scalar_prefetch=1, grid=(T // tm,),
            in_specs=[pl.BlockSpec((tm, D), lambda i, d: (i, 0)),
                      pl.BlockSpec((tm, LANES), lambda i, d: (i, 0)),
                      pl.BlockSpec((1, 1, D), lambda i, d: (i // tpb, 0, 0)),
                      pl.BlockSpec(memory_space=pl.ANY)],
            out_specs=pl.BlockSpec((tm, D), lambda i, d: (i, 0)),
            scratch_shapes=[pltpu.VMEM((TOP_K, tm, D // 2), U32), pltpu.SemaphoreType.DMA(())]),
        compiler_params=_params(("arbitrary",)),
        name="moe_combine",
    )(dest, x2, gates, gt, ys)


def _final_kernel(x_ref, g_ref, o_ref):
    x = x_ref[...]
    var = jnp.mean(x * x, axis=-1, keepdims=True)
    o_ref[...] = x * lax.rsqrt(var + EPS) * g_ref[...]


def _final_norm(x2, g):
    T, D = x2.shape
    tm = _pick(T, 512, 8)
    return pl.pallas_call(
        _final_kernel,
        out_shape=jax.ShapeDtypeStruct((T, D), F32),
        grid=(T // tm,),
        in_specs=[pl.BlockSpec((tm, D), lambda i: (i, 0)), pl.BlockSpec((1, D), lambda i: (0, 0))],
        out_specs=pl.BlockSpec((tm, D), lambda i: (i, 0)),
        compiler_params=_params(("arbitrary",)),
        name="final_norm",
    )(x2, g)


def _routing_tables(route, counts, n_experts, n_blocks):
    bm = MOE_ROWS
    cnt = counts[0, :n_experts].astype(I32)
    padded = (cnt + bm - 1) // bm * bm
    pend = jnp.cumsum(padded)
    pstart = pend - padded
    idx_t = route[:, :TOP_K].T
    base = jnp.zeros_like(idx_t)
    for e in range(n_experts):
        base = jnp.where(idx_t == e, pstart[e], base)
    dest = (base + route[:, TOP_K:2 * TOP_K].T).T.reshape(-1)
    first_row = jnp.arange(n_blocks, dtype=I32) * bm
    block_e = jnp.minimum(jnp.sum((pend[None, :] <= first_row[:, None]).astype(I32), axis=1), n_experts - 1)
    n_used = (pend[-1:] // bm).astype(I32)
    blk = jnp.arange(n_blocks, dtype=I32)
    first = ((blk == 0) | (block_e != jnp.roll(block_e, 1))) & (blk < n_used[0])
    start = jnp.where(first, blk, n_blocks)
    later = jnp.concatenate([lax.cummin(start[::-1])[::-1][1:], jnp.full((1,), n_blocks, I32)])
    nxt = jnp.where(later < n_blocks, block_e[jnp.minimum(later, n_blocks - 1)], -1)
    return dest, block_e, n_used, first.astype(I32), nxt.astype(I32)


def kernel(x, c, w_ada, b_ada, g_norm_mix, w_in, pool_w, pool_scale, conv_w, conv_b, conv_ln_g, conv_ln_b,
           fgate_b, gmlp_ln_g, gmlp_ws, gmlp_bs, w_gate, b_gate, w_branch, w_o, g_norm_ffn, router_w, router_b,
           w_up, b_up, w_down, b_down, g_final):
    B, S, D = x.shape
    L = w_ada.shape[0]
    T = B * S
    G, GD = pool_w.shape[1], pool_w.shape[2]
    P = G * GD
    C = conv_w.shape[2]
    H = fgate_b.shape[1]
    GW = gmlp_ln_g.shape[1]
    A = (w_in.shape[2] - P - 2 * C - H - 2 * GW) // 3
    dh = A // H
    E = router_w.shape[2]
    off_conv, off_q = P, P + 2 * C
    off_f = off_q + 3 * A
    off_gmlp = off_q + 3 * A
    n_blocks = -(-(T * TOP_K + E * (MOE_ROWS - 1)) // MOE_ROWS)

    c_pad = jnp.pad(c, ((0, (-B) % 8), (0, 0)))
    mod = _ada(c_pad, w_ada, b_ada)
    x2 = x.reshape(T, D)

    w_main = jnp.concatenate([w_in[:, :, :off_f], w_in[:, :, off_f + H:]], axis=2).astype(BF16)
    w_f = jnp.pad(w_in[:, :, off_f:off_f + H], ((0, 0), (0, 0), (0, LANES - H))).astype(BF16)
    w_gate_b = w_gate.astype(BF16)
    w_branch_b = w_branch.astype(BF16)
    w_o_b = w_o.astype(BF16)
    b_up3 = b_up.reshape(L * E, 1, -1)
    b_down3 = b_down.reshape(L * E, 1, D)

    for l in range(L):
        sh1, sc1, gt1, sh2, sc2, gt2 = (mod[l, :B, i * D:(i + 1) * D].reshape(B, 1, D) for i in range(6))
        h, z, f = _in_proj(x2, g_norm_mix[l].reshape(1, D), sh1, sc1, w_main, w_f, l, S)
        z3 = z.reshape(B, S, -1)

        y_a = _pool(z3, pool_w[l].astype(BF16), pool_scale[l].reshape(1, P))
        y_b = _conv(z3, off_conv, conv_w[l], conv_b[l].reshape(1, C), conv_ln_g[l].reshape(1, C),
                    conv_ln_b[l].reshape(1, C))
        fb_pad = jnp.pad(fgate_b[l], (0, LANES - H)).reshape(1, LANES)
        fcol = _cumf(f.reshape(B, S, LANES), fb_pad)
        y_c = _attn(z3, off_q, H, dh, fcol)
        bs_b = jnp.broadcast_to(gmlp_bs[l][:, :, None], (G, GMLP_CHUNK, GW // G))
        y_d = _gmlp(z, off_gmlp, GW, gmlp_ln_g[l].reshape(1, GW), gmlp_ws[l], bs_b)

        merged = _merge(h, (y_a.reshape(T, P), y_b.reshape(T, C), y_c.reshape(T, A), y_d), w_gate_b,
                        b_gate[l].reshape(4, 1, D), w_branch_b, l, S)
        x2 = _out_proj(merged, x2, gt1, w_o_b, l, S)

        rw_pad = jnp.pad(router_w[l], ((0, 0), (0, LANES - E))).astype(BF16)
        rb_pad = jnp.pad(router_b[l], (0, LANES - E), constant_values=-1e30).reshape(1, LANES)
        hp, route, gates, counts = _route(x2, g_norm_ffn[l].reshape(1, D), sh2, sc2, rw_pad, rb_pad, S)
        dest, block_e, n_used, first, nxt = _routing_tables(route, counts, E, n_blocks)
        xs = _dispatch(dest, hp, n_blocks * MOE_ROWS)
        ys = _experts(block_e, n_used, first, nxt, xs, w_up, b_up3, w_down, b_down3, l)
        x2 = _combine(dest, x2, gates, gt2, ys, S)

    return _final_norm(x2, g_final.reshape(1, D)).reshape(B, S, D)
```

```python
import functools
import math

import jax
import jax.numpy as jnp
from jax import lax
from jax.experimental import pallas as pl
from jax.experimental.pallas import tpu as pltpu

F32 = jnp.float32
BF16 = jnp.bfloat16
U32 = jnp.uint32
I32 = jnp.int32

EPS = 1e-5
POOL_WINDOWS = (2, 4, 8, 16)
CONV_KERNEL = 31
CONV_HALO = 32
GMLP_CHUNK = 128
TOP_K = 4
SWIGLU_LIMIT = 7.0
SWIGLU_ALPHA = 1.702
LANES = 128
SUBLANES = 8
MOE_ROWS = 256
VMEM_LIMIT = 56 * 1024 * 1024
HIGHEST = lax.Precision.HIGHEST


def _pick(n, target, mult):
    if n <= target:
        return n
    best = None
    for d in range(mult, target + 1, mult):
        if n % d == 0:
            best = d
    assert best is not None, (n, target, mult)
    return best


def _params(sem):
    return pltpu.CompilerParams(dimension_semantics=sem, vmem_limit_bytes=VMEM_LIMIT)


def _sigmoid(x):
    return 1.0 / (1.0 + jnp.exp(-x))


def _ada_kernel(c_ref, w_ref, b_ref, o_ref):
    c = c_ref[...]
    ca = c * _sigmoid(c)
    o_ref[0] = jnp.dot(ca, w_ref[0], preferred_element_type=F32, precision=HIGHEST) + b_ref[0]


def _ada(c_pad, w_ada, b_ada):
    L, D, N = w_ada.shape
    R = c_pad.shape[0]
    tn = _pick(N, 1024, LANES)
    return pl.pallas_call(
        _ada_kernel,
        out_shape=jax.ShapeDtypeStruct((L, R, N), F32),
        grid=(L, N // tn),
        in_specs=[pl.BlockSpec((R, D), lambda l, j: (0, 0)),
                  pl.BlockSpec((1, D, tn), lambda l, j: (l, 0, j)),
                  pl.BlockSpec((1, 1, tn), lambda l, j: (l, 0, j))],
        out_specs=pl.BlockSpec((1, R, tn), lambda l, j: (l, 0, j)),
        compiler_params=_params(("arbitrary", "arbitrary")),
        name="ada_mod",
    )(c_pad, w_ada, b_ada.reshape(L, 1, N))


def _norm_mod(x, g, sc, sh):
    var = jnp.mean(x * x, axis=-1, keepdims=True)
    y = x * lax.rsqrt(var + EPS) * g
    return y * (1.0 + sc) + sh


def _in_kernel(x_ref, g_ref, sh_ref, sc_ref, w_ref, wf_ref, h_ref, z_ref, f_ref):
    @pl.when(pl.program_id(1) == 0)
    def _():
        hb = _norm_mod(x_ref[...], g_ref[...], sc_ref[0], sh_ref[0]).astype(BF16)
        h_ref[...] = hb
        f_ref[...] = jnp.dot(hb, wf_ref[...], preferred_element_type=F32)

    z_ref[...] = jnp.dot(h_ref[...], w_ref[...], preferred_element_type=F32).astype(BF16)


def _in_proj(x2, g, sh, sc, w_main, w_f, l, S):
    T, D = x2.shape
    N = w_main.shape[2]
    tm = _pick(S, 1024, 16)
    tn = _pick(N, 1408, LANES)
    tpb = S // tm
    return pl.pallas_call(
        _in_kernel,
        out_shape=(jax.ShapeDtypeStruct((T, D), BF16), jax.ShapeDtypeStruct((T, N), BF16),
                   jax.ShapeDtypeStruct((T, LANES), F32)),
        grid=(T // tm, N // tn),
        in_specs=[pl.BlockSpec((tm, D), lambda i, j: (i, 0)),
                  pl.BlockSpec((1, D), lambda i, j: (0, 0)),
                  pl.BlockSpec((1, 1, D), lambda i, j: (i // tpb, 0, 0)),
                  pl.BlockSpec((1, 1, D), lambda i, j: (i // tpb, 0, 0)),
                  pl.BlockSpec((None, D, tn), lambda i, j: (l, 0, j)),
                  pl.BlockSpec((None, D, LANES), lambda i, j: (l, 0, 0))],
        out_specs=(pl.BlockSpec((tm, D), lambda i, j: (i, 0)),
                   pl.BlockSpec((tm, tn), lambda i, j: (i, j)),
                   pl.BlockSpec((tm, LANES), lambda i, j: (i, 0))),
        compiler_params=_params(("arbitrary", "arbitrary")),
        name="in_proj",
    )(x2, g, sh, sc, w_main, w_f)


def _pool_kernel(z_ref, w_ref, s_ref, o_ref):
    S = z_ref.shape[1]
    G, GD, _ = w_ref.shape
    row = lax.broadcasted_iota(I32, (S, GD), 0)
    for g in range(G):
        wnd = POOL_WINDOWS[g]
        a = z_ref[0, :, g * GD:(g + 1) * GD].astype(F32)
        s = a
        k = 1
        while k < wnd:
            s = s + jnp.where(row >= k, pltpu.roll(s, k, axis=0), 0.0)
            k *= 2
        cnt = jnp.minimum(row + 1, wnd).astype(F32)
        mixed = s / cnt - a
        y = jnp.dot(mixed.astype(BF16), w_ref[g], preferred_element_type=F32)
        o_ref[0, :, g * GD:(g + 1) * GD] = (y * s_ref[:, g * GD:(g + 1) * GD]).astype(BF16)


def _pool(z3, pool_w, pool_scale):
    B, S, _ = z3.shape
    G, GD, _ = pool_w.shape
    P = G * GD
    return pl.pallas_call(
        _pool_kernel,
        out_shape=jax.ShapeDtypeStruct((B, S, P), BF16),
        grid=(B,),
        in_specs=[pl.BlockSpec((1, S, P), lambda b: (b, 0, 0)),
                  pl.BlockSpec((G, GD, GD), lambda b: (0, 0, 0)),
                  pl.BlockSpec((1, P), lambda b: (0, 0))],
        out_specs=pl.BlockSpec((1, S, P), lambda b: (b, 0, 0)),
        compiler_params=_params(("arbitrary",)),
        name="pool_mixer",
    )(z3, pool_w, pool_scale)


CONV_ROWS = 64


def _conv_kernel(za_ref, zb_ref, ha_ref, hb_ref, w_ref, b_ref, g_ref, beta_ref, o_ref, ext_ref, sh_ref):
    ts = za_ref.shape[1]
    C = za_ref.shape[2]
    halo = ha_ref[0].astype(F32) * _sigmoid(hb_ref[0].astype(F32))
    ext_ref[0:CONV_HALO, :] = jnp.where(pl.program_id(1) > 0, halo, 0.0)
    ext_ref[CONV_HALO:CONV_HALO + ts, :] = za_ref[0].astype(F32) * _sigmoid(zb_ref[0].astype(F32))
    n_sh = CONV_HALO + ts - SUBLANES
    sh_ref[0] = ext_ref[...]
    for b in range(1, SUBLANES):
        sh_ref[b, 0:n_sh, :] = ext_ref[b:b + n_sh, :]
    first = CONV_HALO - (CONV_KERNEL - 1)
    R = min(CONV_ROWS, ts)
    for r0 in range(0, ts, R):
        acc = jnp.broadcast_to(b_ref[...], (R, C))
        for j in range(CONV_KERNEL):
            a, b = divmod(first + j, SUBLANES)
            acc = acc + w_ref[j:j + 1, :] * sh_ref[b, r0 + a * SUBLANES:r0 + a * SUBLANES + R, :]
        mu = jnp.mean(acc, axis=-1, keepdims=True)
        xc = acc - mu
        var = jnp.mean(xc * xc, axis=-1, keepdims=True)
        y = xc * lax.rsqrt(var + EPS) * g_ref[...] + beta_ref[...]
        o_ref[0, r0:r0 + R, :] = (y * _sigmoid(y)).astype(BF16)


def _conv(z3, off_a, conv_w, conv_b, ln_g, ln_b):
    B, S, _ = z3.shape
    K, C = conv_w.shape
    ts = _pick(S, 512, CONV_HALO)
    ia, ib = off_a // C, off_a // C + 1
    hpt = ts // CONV_HALO
    assert off_a % C == 0
    halo_map = lambda col: (lambda b, i: (b, jnp.maximum(i * hpt - 1, 0), col))
    return pl.pallas_call(
        _conv_kernel,
        out_shape=jax.ShapeDtypeStruct((B, S, C), BF16),
        grid=(B, S // ts),
        in_specs=[pl.BlockSpec((1, ts, C), lambda b, i: (b, i, ia)),
                  pl.BlockSpec((1, ts, C), lambda b, i: (b, i, ib)),
                  pl.BlockSpec((1, CONV_HALO, C), halo_map(ia)),
                  pl.BlockSpec((1, CONV_HALO, C), halo_map(ib)),
                  pl.BlockSpec((K, C), lambda b, i: (0, 0)),
                  pl.BlockSpec((1, C), lambda b, i: (0, 0)),
                  pl.BlockSpec((1, C), lambda b, i: (0, 0)),
                  pl.BlockSpec((1, C), lambda b, i: (0, 0))],
        out_specs=pl.BlockSpec((1, ts, C), lambda b, i: (b, i, 0)),
        scratch_shapes=[pltpu.VMEM((CONV_HALO + ts, C), F32),
                        pltpu.VMEM((SUBLANES, CONV_HALO + ts, C), F32)],
        compiler_params=_params(("arbitrary", "arbitrary")),
        name="conv_mixer",
    )(z3, z3, z3, z3, conv_w, conv_b, ln_g, ln_b)


CUM_CHUNK = 256


def _cumf_kernel(f_ref, b_ref, fc_ref):
    S = f_ref.shape[1]
    C = min(CUM_CHUNK, S)
    x = f_ref[0] + b_ref[...]
    ls = -(jnp.maximum(-x, 0.0) + jnp.log1p(jnp.exp(-jnp.abs(x))))
    tri = (lax.broadcasted_iota(I32, (C, C), 0) >= lax.broadcasted_iota(I32, (C, C), 1)).astype(F32)
    carry = jnp.zeros((1, LANES), F32)
    for c in range(S // C):
        cs = jnp.dot(tri, ls[c * C:(c + 1) * C], preferred_element_type=F32, precision=HIGHEST) + carry
        fc_ref[0, c * C:(c + 1) * C, :] = cs
        carry = cs[C - 1:C, :]


def _cumf(f3, fb_pad):
    B, S, _ = f3.shape
    return pl.pallas_call(
        _cumf_kernel,
        out_shape=jax.ShapeDtypeStruct((B, S, LANES), F32),
        grid=(B,),
        in_specs=[pl.BlockSpec((1, S, LANES), lambda b: (b, 0, 0)),
                  pl.BlockSpec((1, LANES), lambda b: (0, 0))],
        out_specs=pl.BlockSpec((1, S, LANES), lambda b: (b, 0, 0)),
        compiler_params=_params(("arbitrary",)),
        name="forget_cumsum",
    )(f3, fb_pad)


ATTN_STRIP = 32
LOG2E = 1.4426950408889634


def _split3(x):
    hi = x.astype(BF16).astype(F32)
    mid = (x - hi).astype(BF16).astype(F32)
    return hi, mid, x - hi - mid


def _attn_kernel(q_ref, k_ref, v_ref, fc_ref, o_ref, ka_ref, fh_ref, qa_ref, s_ref, p_ref, m_ref, l_ref, acc_ref,
                 *, scale):
    h = pl.program_id(1)
    i = pl.program_id(2)
    tq = q_ref.shape[1]
    S = k_ref.shape[1]
    nl = tq // LANES
    R = ATTN_STRIP
    q_start = pl.multiple_of(i * tq, tq)

    @pl.when(i == 0)
    def _():
        lane = lax.broadcasted_iota(I32, (S, LANES), 1)
        fh = jnp.sum(jnp.where(lane == h, fc_ref[0], 0.0), axis=1, keepdims=True) * LOG2E
        fh_ref[...] = jnp.broadcast_to(fh, (S, LANES))
        hi, mid, lo = _split3(fh)
        aug = jnp.where(lane == 0, hi, jnp.where(lane == 1, mid, jnp.where(lane == 2, lo,
                                                                           jnp.where(lane < 6, 1.0, 0.0))))
        ka_ref[:, 0:LANES] = k_ref[0]
        ka_ref[:, LANES:2 * LANES] = aug.astype(BF16)

    hi0, mid0, lo0 = _split3(fh_ref[pl.ds(q_start, SUBLANES), :][0:1, :])
    lane1 = lax.broadcasted_iota(I32, (1, LANES), 1)
    qrow = jnp.where(lane1 < 3, -1.0, jnp.where(lane1 == 3, hi0, jnp.where(lane1 == 4, mid0,
                                                                           jnp.where(lane1 == 5, lo0, 0.0))))
    qa_ref[:, 0:LANES] = (q_ref[0].astype(F32) * (scale * LOG2E)).astype(BF16)
    qa_ref[:, LANES:2 * LANES] = jnp.broadcast_to(qrow, (tq, LANES)).astype(BF16)
    m_ref[...] = jnp.full(m_ref.shape, -jnp.inf, F32)
    l_ref[...] = jnp.zeros(l_ref.shape, F32)
    acc_ref[...] = jnp.zeros(acc_ref.shape, F32)

    def logits(j, slot):
        start = pl.multiple_of(j * tq, tq)
        s_ref[slot] = lax.dot_general(qa_ref[...], ka_ref[pl.ds(start, tq), :], (((1,), (1,)), ((), ())),
                                      preferred_element_type=F32)

    def softmax_pv(j, slot, masked):
        start = pl.multiple_of(j * tq, tq)

        def scaled(r0, c):
            t = s_ref[slot, r0:r0 + R, c * LANES:(c + 1) * LANES]
            if masked:
                row = lax.broadcasted_iota(I32, (R, LANES), 0) + r0
                col = lax.broadcasted_iota(I32, (R, LANES), 1) + c * LANES
                t = jnp.where(col <= row, t, -jnp.inf)
            return t

        def live(r0):
            return [c for c in range(nl) if not masked or c * LANES < r0 + R]

        for r0 in range(0, tq, R):
            mx = scaled(r0, 0)
            for c in live(r0)[1:]:
                mx = jnp.maximum(mx, scaled(r0, c))
            m_old = m_ref[r0:r0 + R, :]
            m_new = jnp.maximum(m_old, jnp.max(mx, axis=1, keepdims=True))
            alpha = jnp.exp2(m_old - m_new)
            m_ref[r0:r0 + R, :] = m_new
            l_ref[r0:r0 + R, :] = alpha * l_ref[r0:r0 + R, :]
            acc_ref[r0:r0 + R, :] = alpha * acc_ref[r0:r0 + R, :]
        for r0 in range(0, tq, R):
            m_new = m_ref[r0:r0 + R, :]
            rs = None
            for c in range(nl):
                if c in live(r0):
                    p = jnp.exp2(scaled(r0, c) - m_new)
                    rs = p if rs is None else rs + p
                else:
                    p = jnp.zeros((R, LANES), F32)
                p_ref[r0:r0 + R, c * LANES:(c + 1) * LANES] = p.astype(BF16)
            l_ref[r0:r0 + R, :] += rs
        acc_ref[...] += jnp.dot(p_ref[...], v_ref[0, pl.ds(start, tq), :], preferred_element_type=F32)

    logits(0, 0)

    def pair(k, carry):
        logits(2 * k + 1, 1)
        softmax_pv(2 * k, 0, False)
        logits(2 * k + 2, 0)
        softmax_pv(2 * k + 1, 1, False)
        return carry

    lax.fori_loop(0, i // 2, pair, 0)

    @pl.when(i % 2 == 1)
    def _():
        logits(i, 1)
        softmax_pv(i - 1, 0, False)
        softmax_pv(i, 1, True)

    @pl.when(i % 2 == 0)
    def _():
        softmax_pv(i, 0, True)

    o_ref[0] = (acc_ref[...] / jnp.sum(l_ref[...], axis=1, keepdims=True)).astype(BF16)


def _attn(z3, off_q, H, dh, fcol):
    B, S, _ = z3.shape
    tq = _pick(S, 512, LANES)
    iq, ik, iv = off_q // dh, off_q // dh + H, off_q // dh + 2 * H
    assert off_q % dh == 0 and dh == LANES and tq % ATTN_STRIP == 0
    return pl.pallas_call(
        functools.partial(_attn_kernel, scale=1.0 / math.sqrt(dh)),
        out_shape=jax.ShapeDtypeStruct((B, S, H * dh), BF16),
        grid=(B, H, S // tq),
        in_specs=[pl.BlockSpec((1, tq, dh), lambda b, h, i: (b, i, iq + h)),
                  pl.BlockSpec((1, S, dh), lambda b, h, i: (b, 0, ik + h)),
                  pl.BlockSpec((1, S, dh), lambda b, h, i: (b, 0, iv + h)),
                  pl.BlockSpec((1, S, LANES), lambda b, h, i: (b, 0, 0))],
        out_specs=pl.BlockSpec((1, tq, dh), lambda b, h, i: (b, i, h)),
        scratch_shapes=[pltpu.VMEM((S, 2 * LANES), BF16), pltpu.VMEM((S, LANES), F32),
                        pltpu.VMEM((tq, 2 * LANES), BF16), pltpu.VMEM((2, tq, tq), F32),
                        pltpu.VMEM((tq, tq), BF16), pltpu.VMEM((tq, LANES), F32), pltpu.VMEM((tq, LANES), F32),
                        pltpu.VMEM((tq, dh), F32)],
        compiler_params=_params(("arbitrary", "arbitrary", "arbitrary")),
        name="forget_attn",
    )(z3, z3, z3, fcol)


def _gelu_tanh(x):
    return 0.5 * x * (1.0 + jnp.tanh(math.sqrt(2.0 / math.pi) * (x + 0.044715 * (x * x * x))))


def _gmlp_kernel(u_ref, v_ref, g_ref, ws_ref, bs_ref, o_ref):
    ts, GW = u_ref.shape
    G = ws_ref.shape[0]
    GD = GW // G
    ck = GMLP_CHUNK
    u = _gelu_tanh(u_ref[...].astype(F32))
    v = _gelu_tanh(v_ref[...].astype(F32))
    var = jnp.mean(v * v, axis=-1, keepdims=True)
    vb = (v * lax.rsqrt(var + EPS) * g_ref[...]).astype(BF16)
    causal = lax.broadcasted_iota(I32, (ck, ck), 0) >= lax.broadcasted_iota(I32, (ck, ck), 1)
    for g in range(G):
        wm = jnp.where(causal, ws_ref[g], 0.0).astype(BF16)
        for c in range(ts // ck):
            sv = jnp.dot(wm, vb[c * ck:(c + 1) * ck, g * GD:(g + 1) * GD], preferred_element_type=F32)
            sv = sv + bs_ref[g]
            o_ref[c * ck:(c + 1) * ck, g * GD:(g + 1) * GD] = (
                u[c * ck:(c + 1) * ck, g * GD:(g + 1) * GD] * sv).astype(BF16)


def _gmlp(z2, off_u, GW, ln_g, ws, bs_b):
    T, _ = z2.shape
    G = ws.shape[0]
    ts = _pick(T, 512, GMLP_CHUNK)
    iu = off_u // GW
    assert off_u % GW == 0
    return pl.pallas_call(
        _gmlp_kernel,
        out_shape=jax.ShapeDtypeStruct((T, GW), BF16),
        grid=(T // ts,),
        in_specs=[pl.BlockSpec((ts, GW), lambda i: (i, iu)),
                  pl.BlockSpec((ts, GW), lambda i: (i, iu + 1)),
                  pl.BlockSpec((1, GW), lambda i: (0, 0)),
                  pl.BlockSpec((G, GMLP_CHUNK, GMLP_CHUNK), lambda i: (0, 0, 0)),
                  pl.BlockSpec((G, GMLP_CHUNK, GW // G), lambda i: (0, 0, 0))],
        out_specs=pl.BlockSpec((ts, GW), lambda i: (i, 0)),
        compiler_params=_params(("arbitrary",)),
        name="gmlp_mixer",
    )(z2, z2, ln_g, ws, bs_b)


def _merge_kernel(h_ref, ya_ref, yb_ref, yc_ref, yd_ref, wg0_ref, wg1_ref, wg2_ref, wg3_ref, bg_ref,
                  wa_ref, wb_ref, wc_ref, wd_ref, o_ref):
    h = h_ref[...]
    merged = None
    for br, (y_ref, wg_ref, wbr_ref) in enumerate(((ya_ref, wg0_ref, wa_ref), (yb_ref, wg1_ref, wb_ref),
                                                   (yc_ref, wg2_ref, wc_ref), (yd_ref, wg3_ref, wd_ref))):
        gate = _sigmoid(jnp.dot(h, wg_ref[...], preferred_element_type=F32) + bg_ref[br])
        term = gate * jnp.dot(y_ref[...], wbr_ref[...], preferred_element_type=F32)
        merged = term if merged is None else merged + term
    o_ref[...] = merged.astype(BF16)


def _merge(h, ys, w_gate, b_gate4, w_branch, l, S):
    T, D = h.shape
    tm = _pick(S, 512, 16)
    tn = _pick(D, 512, LANES)
    nb = D // tn
    row = lambda width: pl.BlockSpec((tm, width), lambda i, n: (i, 0))
    wg = lambda br: pl.BlockSpec((None, D, tn), lambda i, n: (l, 0, br * nb + n))
    wb_specs, off = [], 0
    for y in ys:
        rows = y.shape[1]
        assert off % rows == 0
        wb_specs.append(pl.BlockSpec((None, rows, tn), lambda i, n, rb=off // rows: (l, rb, n)))
        off += rows
    return pl.pallas_call(
        _merge_kernel,
        out_shape=jax.ShapeDtypeStruct((T, D), BF16),
        grid=(T // tm, nb),
        in_specs=[row(D)] + [row(y.shape[1]) for y in ys] + [wg(0), wg(1), wg(2), wg(3),
                  pl.BlockSpec((4, 1, tn), lambda i, n: (0, 0, n))] + wb_specs,
        out_specs=pl.BlockSpec((tm, tn), lambda i, n: (i, n)),
        compiler_params=_params(("arbitrary", "arbitrary")),
        name="gated_merge",
    )(h, *ys, w_gate, w_gate, w_gate, w_gate, b_gate4, w_branch, w_branch, w_branch, w_branch)


def _out_kernel(m_ref, x_ref, gt_ref, wo_ref, o_ref):
    b = pl.program_id(0) // (pl.num_programs(0) // gt_ref.shape[0])
    o_ref[...] = x_ref[...] + gt_ref[b] * jnp.dot(m_ref[...], wo_ref[...], preferred_element_type=F32)


def _out_proj(merged, x2, gt, w_o, l, S):
    T, D = x2.shape
    tm = _pick(S, 512, 16)
    B = gt.shape[0]
    return pl.pallas_call(
        _out_kernel,
        out_shape=jax.ShapeDtypeStruct((T, D), F32),
        grid=(T // tm,),
        in_specs=[pl.BlockSpec((tm, D), lambda i: (i, 0)),
                  pl.BlockSpec((tm, D), lambda i: (i, 0)),
                  pl.BlockSpec((B, 1, D), lambda i: (0, 0, 0)),
                  pl.BlockSpec((None, D, D), lambda i: (l, 0, 0))],
        out_specs=pl.BlockSpec((tm, D), lambda i: (i, 0)),
        compiler_params=_params(("arbitrary",)),
        name="out_proj",
    )(merged, x2, gt, w_o)


def _pack_bf16_pairs(hb):
    W = hb.shape[1] // 2
    lo = pltpu.bitcast(hb[:, :W].astype(F32), U32)
    hi = pltpu.bitcast(hb[:, W:].astype(F32), U32)
    return (hi & jnp.uint32(0xFFFF0000)) | (lo >> 16)


def _unpack_bf16_pairs(p):
    lo = pltpu.bitcast(p << 16, F32)
    hi = pltpu.bitcast(p & jnp.uint32(0xFFFF0000), F32)
    return lo, hi


def _route_kernel(x_ref, g_ref, sh_ref, sc_ref, rw_ref, rb_ref, hp_ref, route_ref, gates_ref, cnt_ref, carry_ref):
    i = pl.program_id(0)
    tm = x_ref.shape[0]

    @pl.when(i == 0)
    def _():
        carry_ref[...] = jnp.zeros_like(carry_ref)

    hb = _norm_mod(x_ref[...], g_ref[...], sc_ref[0], sh_ref[0]).astype(BF16)
    hp_ref[...] = _pack_bf16_pairs(hb)
    logits = jnp.dot(hb, rw_ref[...], preferred_element_type=F32) + rb_ref[...]
    lane = lax.broadcasted_iota(I32, (tm, LANES), 1)
    lanef = lane.astype(F32)
    vals, sels = [], []
    for _ in range(TOP_K):
        m = jnp.max(logits, axis=1, keepdims=True)
        idx = jnp.min(jnp.where(logits == m, lanef, float(LANES)), axis=1, keepdims=True)
        sel = lanef == idx
        logits = jnp.where(sel, -jnp.inf, logits)
        vals.append(m)
        sels.append(sel)
    exps = [jnp.exp(v - vals[0]) for v in vals]
    denom = exps[0] + exps[1] + exps[2] + exps[3]
    onehot = jnp.zeros((tm, LANES), F32)
    for sel in sels:
        onehot = jnp.where(sel, 1.0, onehot)
    strict = (lax.broadcasted_iota(I32, (tm, tm), 0) > lax.broadcasted_iota(I32, (tm, tm), 1)).astype(BF16)
    before = jnp.dot(strict, onehot.astype(BF16), preferred_element_type=F32) + carry_ref[...]
    route = jnp.zeros((tm, LANES), F32)
    gates = jnp.zeros((tm, LANES), F32)
    for k in range(TOP_K):
        idx_k = jnp.sum(jnp.where(sels[k], lanef, 0.0), axis=1, keepdims=True)
        pos_k = jnp.sum(jnp.where(sels[k], before, 0.0), axis=1, keepdims=True)
        route = jnp.where(lane == k, idx_k, route)
        route = jnp.where(lane == TOP_K + k, pos_k, route)
        gates = jnp.where(lane == k, exps[k] / denom, gates)
    route_ref[...] = route.astype(I32)
    gates_ref[...] = gates
    carry_ref[...] += jnp.sum(onehot, axis=0, keepdims=True)
    cnt_ref[...] = carry_ref[...]


def _route(x2, g, sh, sc, rw_pad, rb_pad, S):
    T, D = x2.shape
    tm = _pick(S, 512, 16)
    tpb = S // tm
    return pl.pallas_call(
        _route_kernel,
        out_shape=(jax.ShapeDtypeStruct((T, D // 2), U32), jax.ShapeDtypeStruct((T, LANES), I32),
                   jax.ShapeDtypeStruct((T, LANES), F32), jax.ShapeDtypeStruct((1, LANES), F32)),
        grid=(T // tm,),
        in_specs=[pl.BlockSpec((tm, D), lambda i: (i, 0)),
                  pl.BlockSpec((1, D), lambda i: (0, 0)),
                  pl.BlockSpec((1, 1, D), lambda i: (i // tpb, 0, 0)),
                  pl.BlockSpec((1, 1, D), lambda i: (i // tpb, 0, 0)),
                  pl.BlockSpec((D, LANES), lambda i: (0, 0)),
                  pl.BlockSpec((1, LANES), lambda i: (0, 0))],
        out_specs=(pl.BlockSpec((tm, D // 2), lambda i: (i, 0)),
                   pl.BlockSpec((tm, LANES), lambda i: (i, 0)),
                   pl.BlockSpec((tm, LANES), lambda i: (i, 0)),
                   pl.BlockSpec((1, LANES), lambda i: (0, 0))),
        scratch_shapes=[pltpu.VMEM((1, LANES), F32)],
        compiler_params=_params(("arbitrary",)),
        name="moe_route",
    )(x2, g, sh, sc, rw_pad, rb_pad)


def _dispatch_kernel(dest_ref, h_ref, xs_ref, sem):
    tm = h_ref.shape[0]
    base = pl.program_id(0) * (tm * TOP_K)

    def row_copy(r, d):
        return pltpu.make_async_copy(h_ref.at[pl.ds(r, 1)], xs_ref.at[pl.ds(d, 1)], sem)

    def issue(r, _):
        for k in range(TOP_K):
            row_copy(r, dest_ref[base + r * TOP_K + k]).start(priority=k % 2)
        return 0

    def drain(r, _):
        for k in range(TOP_K):
            row_copy(r, dest_ref[base + r * TOP_K + k]).wait()
        return 0

    lax.fori_loop(0, tm, issue, 0)
    lax.fori_loop(0, tm, drain, 0)


def _dispatch(dest, hp, n_rows):
    T, W = hp.shape
    tm = _pick(T, 512, 8)
    return pl.pallas_call(
        _dispatch_kernel,
        out_shape=jax.ShapeDtypeStruct((n_rows, W), U32),
        grid_spec=pltpu.PrefetchScalarGridSpec(
            num_scalar_prefetch=1, grid=(T // tm,),
            in_specs=[pl.BlockSpec((tm, W), lambda i, d: (i, 0))],
            out_specs=pl.BlockSpec(memory_space=pl.ANY),
            scratch_shapes=[pltpu.SemaphoreType.DMA(())]),
        compiler_params=_params(("arbitrary",)),
        name="moe_dispatch",
    )(dest, hp)


CAST_ROWS = 256


def _expert_kernel(be_ref, nu_ref, first_ref, nxt_ref, x_ref, wu_hbm, bu_ref, wd_hbm, bd_ref, o_ref,
                   su_ref, sd_ref, wub_ref, wdb_ref, sem, *, layer):
    b = pl.program_id(0)

    def fetch(e):
        return (pltpu.make_async_copy(wu_hbm.at[layer, e], su_ref, sem.at[0]),
                pltpu.make_async_copy(wd_hbm.at[layer, e], sd_ref, sem.at[1]))

    @pl.when(b == 0)
    def _():
        for cp in fetch(be_ref[0]):
            cp.start()

    @pl.when((b < nu_ref[0]) & (first_ref[b] == 1))
    def _():
        for cp in fetch(be_ref[b]):
            cp.wait()

        def cast(src, dst):
            chunk = _pick(src.shape[0], CAST_ROWS, 16)

            def step(r, _):
                rows = pl.ds(pl.multiple_of(r * chunk, chunk), chunk)
                dst[rows, :] = src[rows, :].astype(BF16)
                return 0
            lax.fori_loop(0, src.shape[0] // chunk, step, 0)

        cast(su_ref, wub_ref)
        cast(sd_ref, wdb_ref)

        @pl.when(nxt_ref[b] >= 0)
        def _():
            for cp in fetch(nxt_ref[b]):
                cp.start()

    @pl.when(b < nu_ref[0])
    def _():
        lo, hi = _unpack_bf16_pairs(x_ref[...])
        xb = jnp.concatenate([lo.astype(BF16), hi.astype(BF16)], axis=1)
        gu = jnp.dot(xb, wub_ref[...], preferred_element_type=F32) + bu_ref[0]
        F = gu.shape[1] // 2
        g = jnp.minimum(gu[:, :F], SWIGLU_LIMIT)
        lin = jnp.clip(gu[:, F:], -SWIGLU_LIMIT, SWIGLU_LIMIT)
        act = g * _sigmoid(SWIGLU_ALPHA * g) * (lin + 1.0)
        out = jnp.dot(act.astype(BF16), wdb_ref[...], preferred_element_type=F32) + bd_ref[0]
        o_ref[...] = _pack_bf16_pairs(out.astype(BF16))


def _experts(block_e, n_used, first, nxt, xs, w_up, b_up, w_down, b_down, l):
    NP, W = xs.shape
    _, E, D, F2 = w_up.shape
    bm = MOE_ROWS
    blk = lambda i, be, nu, fi, nx: (jnp.minimum(i, nu[0] - 1), 0)
    return pl.pallas_call(
        functools.partial(_expert_kernel, layer=l),
        out_shape=jax.ShapeDtypeStruct((NP, W), U32),
        grid_spec=pltpu.PrefetchScalarGridSpec(
            num_scalar_prefetch=4, grid=(NP // bm,),
            in_specs=[pl.BlockSpec((bm, W), blk),
                      pl.BlockSpec(memory_space=pl.ANY),
                      pl.BlockSpec((1, 1, F2), lambda i, be, nu, fi, nx: (l * E + be[i], 0, 0)),
                      pl.BlockSpec(memory_space=pl.ANY),
                      pl.BlockSpec((1, 1, D), lambda i, be, nu, fi, nx: (l * E + be[i], 0, 0))],
            out_specs=pl.BlockSpec((bm, W), blk),
            scratch_shapes=[pltpu.VMEM((D, F2), F32), pltpu.VMEM((F2 // 2, D), F32),
                            pltpu.VMEM((D, F2), BF16), pltpu.VMEM((F2 // 2, D), BF16),
                            pltpu.SemaphoreType.DMA((2,))]),
        compiler_params=_params(("arbitrary",)),
        name="moe_experts",
    )(block_e, n_used, first, nxt, xs, w_up, b_up, w_down, b_down)


def _combine_kernel(dest_ref, x_ref, gates_ref, gt_ref, ys_ref, o_ref, buf_ref, sem, g_ref=None):
    tm = x_ref.shape[0]
    base = pl.program_id(0) * (tm * TOP_K)

    def row_copy(r, k, d):
        return pltpu.make_async_copy(ys_ref.at[pl.ds(d, 1)], buf_ref.at[k, pl.ds(r, 1)], sem)

    def issue(r, _):
        for k in range(TOP_K):
            row_copy(r, k, dest_ref[base + r * TOP_K + k]).start(priority=k % 2)
        return 0

    def drain(r, _):
        for k in range(TOP_K):
            row_copy(r, k, dest_ref[base + r * TOP_K + k]).wait()
        return 0

    lax.fori_loop(0, tm, issue, 0)
    lax.fori_loop(0, tm, drain, 0)
    lane = lax.broadcasted_iota(I32, (tm, LANES), 1)
    gates = gates_ref[...]
    W = buf_ref.shape[2]
    y_lo = jnp.zeros((tm, W), F32)
    y_hi = jnp.zeros((tm, W), F32)
    for k in range(TOP_K):
        gk = jnp.sum(jnp.where(lane == k, gates, 0.0), axis=1, keepdims=True)
        lo, hi = _unpack_bf16_pairs(buf_ref[k])
        y_lo = y_lo + gk * lo
        y_hi = y_hi + gk * hi
    gt = gt_ref[0]
    x_lo = x_ref[:, :W] + gt[:, :W] * y_lo
    x_hi = x_ref[:, W:] + gt[:, W:] * y_hi
    if g_ref is not None:
        ssq = jnp.sum(x_lo * x_lo, axis=-1, keepdims=True) + jnp.sum(x_hi * x_hi, axis=-1, keepdims=True)
        inv = lax.rsqrt(ssq / (2 * W) + EPS)
        x_lo = x_lo * inv * g_ref[:, :W]
        x_hi = x_hi * inv * g_ref[:, W:]
    o_ref[:, :W] = x_lo
    o_ref[:, W:] = x_hi


def _combine_final_kernel(dest_ref, x_ref, gates_ref, gt_ref, ys_ref, g_ref, o_ref, buf_ref, sem):
    _combine_kernel(dest_ref, x_ref, gates_ref, gt_ref, ys_ref, o_ref, buf_ref, sem, g_ref=g_ref)


def _combine(dest, x2, gates, gt, ys, S, g_final=None):
    T, D = x2.shape
    tm = _pick(S, 256, 8)
    tpb = S // tm
    final = g_final is not None
    in_specs = [pl.BlockSpec((tm, D), lambda i, d: (i, 0)),
                pl.BlockSpec((tm, LANES), lambda i, d: (i, 0)),
                pl.BlockSpec((1, 1, D), lambda i, d: (i // tpb, 0, 0)),
                pl.BlockSpec(memory_space=pl.ANY)]
    if final:
        in_specs.append(pl.BlockSpec((1, D), lambda i, d: (0, 0)))
    return pl.pallas_call(
        _combine_final_kernel if final else _combine_kernel,
        out_shape=jax.ShapeDtypeStruct((T, D), F32),
        grid_spec=pltpu.PrefetchScalarGridSpec(
            num_scalar_prefetch=1, grid=(T // tm,),
            in_specs=in_specs,
            out_specs=pl.BlockSpec((tm, D), lambda i, d: (i, 0)),
            scratch_shapes=[pltpu.VMEM((TOP_K, tm, D // 2), U32), pltpu.SemaphoreType.DMA(())]),
        compiler_params=_params(("arbitrary",)),
        name="moe_combine_final" if final else "moe_combine",
    )(dest, x2, gates, gt, ys, *((g_final,) if final else ()))


def _routing_tables(route, counts, n_experts, n_blocks):
    bm = MOE_ROWS
    cnt = counts[0, :n_experts].astype(I32)
    padded = (cnt + bm - 1) // bm * bm
    pend = jnp.cumsum(padded)
    pstart = pend - padded
    idx_t = route[:, :TOP_K].T
    base = jnp.zeros_like(idx_t)
    for e in range(n_experts):
        base = jnp.where(idx_t == e, pstart[e], base)
    dest = (base + route[:, TOP_K:2 * TOP_K].T).T.reshape(-1)
    first_row = jnp.arange(n_blocks, dtype=I32) * bm
    block_e = jnp.minimum(jnp.sum((pend[None, :] <= first_row[:, None]).astype(I32), axis=1), n_experts - 1)
    n_used = (pend[-1:] // bm).astype(I32)
    blk = jnp.arange(n_blocks, dtype=I32)
    first = ((blk == 0) | (block_e != jnp.roll(block_e, 1))) & (blk < n_used[0])
    start = jnp.where(first, blk, n_blocks)
    later = jnp.concatenate([lax.cummin(start[::-1])[::-1][1:], jnp.full((1,), n_blocks, I32)])
    nxt = jnp.where(later < n_blocks, block_e[jnp.minimum(later, n_blocks - 1)], -1)
    return dest, block_e, n_used, first.astype(I32), nxt.astype(I32)


def kernel(x, c, w_ada, b_ada, g_norm_mix, w_in, pool_w, pool_scale, conv_w, conv_b, conv_ln_g, conv_ln_b,
           fgate_b, gmlp_ln_g, gmlp_ws, gmlp_bs, w_gate, b_gate, w_branch, w_o, g_norm_ffn, router_w, router_b,
           w_up, b_up, w_down, b_down, g_final):
    B, S, D = x.shape
    L = w_ada.shape[0]
    T = B * S
    G, GD = pool_w.shape[1], pool_w.shape[2]
    P = G * GD
    C = conv_w.shape[2]
    H = fgate_b.shape[1]
    GW = gmlp_ln_g.shape[1]
    A = (w_in.shape[2] - P - 2 * C - H - 2 * GW) // 3
    dh = A // H
    E = router_w.shape[2]
    off_conv, off_q = P, P + 2 * C
    off_f = off_q + 3 * A
    off_gmlp = off_q + 3 * A
    n_blocks = -(-(T * TOP_K + E * (MOE_ROWS - 1)) // MOE_ROWS)

    c_pad = jnp.pad(c, ((0, (-B) % 8), (0, 0)))
    mod = _ada(c_pad, w_ada, b_ada)
    x2 = x.reshape(T, D)

    w_main = jnp.concatenate([w_in[:, :, :off_f], w_in[:, :, off_f + H:]], axis=2).astype(BF16)
    w_f = jnp.pad(w_in[:, :, off_f:off_f + H], ((0, 0), (0, 0), (0, LANES - H))).astype(BF16)
    w_gate_b = w_gate.astype(BF16)
    w_branch_b = w_branch.astype(BF16)
    w_o_b = w_o.astype(BF16)
    b_up3 = b_up.reshape(L * E, 1, -1)
    b_down3 = b_down.reshape(L * E, 1, D)

    for l in range(L):
        sh1, sc1, gt1, sh2, sc2, gt2 = (mod[l, :B, i * D:(i + 1) * D].reshape(B, 1, D) for i in range(6))
        h, z, f = _in_proj(x2, g_norm_mix[l].reshape(1, D), sh1, sc1, w_main, w_f, l, S)
        z3 = z.reshape(B, S, -1)

        y_a = _pool(z3, pool_w[l].astype(BF16), pool_scale[l].reshape(1, P))
        y_b = _conv(z3, off_conv, conv_w[l], conv_b[l].reshape(1, C), conv_ln_g[l].reshape(1, C),
                    conv_ln_b[l].reshape(1, C))
        fb_pad = jnp.pad(fgate_b[l], (0, LANES - H)).reshape(1, LANES)
        fcol = _cumf(f.reshape(B, S, LANES), fb_pad)
        y_c = _attn(z3, off_q, H, dh, fcol)
        bs_b = jnp.broadcast_to(gmlp_bs[l][:, :, None], (G, GMLP_CHUNK, GW // G))
        y_d = _gmlp(z, off_gmlp, GW, gmlp_ln_g[l].reshape(1, GW), gmlp_ws[l], bs_b)

        merged = _merge(h, (y_a.reshape(T, P), y_b.reshape(T, C), y_c.reshape(T, A), y_d), w_gate_b,
                        b_gate[l].reshape(4, 1, D), w_branch_b, l, S)
        x2 = _out_proj(merged, x2, gt1, w_o_b, l, S)

        rw_pad = jnp.pad(router_w[l], ((0, 0), (0, LANES - E))).astype(BF16)
        rb_pad = jnp.pad(router_b[l], (0, LANES - E), constant_values=-1e30).reshape(1, LANES)
        hp, route, gates, counts = _route(x2, g_norm_ffn[l].reshape(1, D), sh2, sc2, rw_pad, rb_pad, S)
        dest, block_e, n_used, first, nxt = _routing_tables(route, counts, E, n_blocks)
        xs = _dispatch(dest, hp, n_blocks * MOE_ROWS)
        ys = _experts(block_e, n_used, first, nxt, xs, w_up, b_up3, w_down, b_down3, l)
        x2 = _combine(dest, x2, gates, gt2, ys, S, g_final.reshape(1, D) if l == L - 1 else None)

    return x2.reshape(B, S, D)
```

```python
import functools
import math

import jax
import jax.numpy as jnp
from jax import lax
from jax.experimental import pallas as pl
from jax.experimental.pallas import tpu as pltpu

F32 = jnp.float32
BF16 = jnp.bfloat16
U32 = jnp.uint32
I32 = jnp.int32

EPS = 1e-5
POOL_WINDOWS = (2, 4, 8, 16)
CONV_KERNEL = 31
CONV_HALO = 32
GMLP_CHUNK = 128
TOP_K = 4
SWIGLU_LIMIT = 7.0
SWIGLU_ALPHA = 1.702
LANES = 128
SUBLANES = 8
MOE_ROWS = 256
VMEM_LIMIT = 56 * 1024 * 1024
HIGHEST = lax.Precision.HIGHEST


def _pick(n, target, mult):
    if n <= target:
        return n
    best = None
    for d in range(mult, target + 1, mult):
        if n % d == 0:
            best = d
    assert best is not None, (n, target, mult)
    return best


def _params(sem):
    return pltpu.CompilerParams(dimension_semantics=sem, vmem_limit_bytes=VMEM_LIMIT)


def _sigmoid(x):
    return 1.0 / (1.0 + jnp.exp(-x))


def _ada_kernel(c_ref, w_ref, b_ref, o_ref):
    c = c_ref[...]
    ca = c * _sigmoid(c)
    o_ref[0] = jnp.dot(ca, w_ref[0], preferred_element_type=F32, precision=HIGHEST) + b_ref[0]


def _ada(c_pad, w_ada, b_ada):
    L, D, N = w_ada.shape
    R = c_pad.shape[0]
    tn = _pick(N, 1024, LANES)
    return pl.pallas_call(
        _ada_kernel,
        out_shape=jax.ShapeDtypeStruct((L, R, N), F32),
        grid=(L, N // tn),
        in_specs=[pl.BlockSpec((R, D), lambda l, j: (0, 0)),
                  pl.BlockSpec((1, D, tn), lambda l, j: (l, 0, j)),
                  pl.BlockSpec((1, 1, tn), lambda l, j: (l, 0, j))],
        out_specs=pl.BlockSpec((1, R, tn), lambda l, j: (l, 0, j)),
        compiler_params=_params(("arbitrary", "arbitrary")),
        name="ada_mod",
    )(c_pad, w_ada, b_ada.reshape(L, 1, N))


def _norm_mod(x, g, sc, sh):
    var = jnp.mean(x * x, axis=-1, keepdims=True)
    y = x * lax.rsqrt(var + EPS) * g
    return y * (1.0 + sc) + sh


def _in_kernel(x_ref, g_ref, sh_ref, sc_ref, w_ref, wf_ref, h_ref, z_ref, f_ref):
    @pl.when(pl.program_id(1) == 0)
    def _():
        hb = _norm_mod(x_ref[...], g_ref[...], sc_ref[0], sh_ref[0]).astype(BF16)
        h_ref[...] = hb
        f_ref[...] = jnp.dot(hb, wf_ref[...], preferred_element_type=F32)

    z_ref[...] = jnp.dot(h_ref[...], w_ref[...], preferred_element_type=F32).astype(BF16)


def _in_proj(x2, g, sh, sc, w_main, w_f, l, S):
    T, D = x2.shape
    N = w_main.shape[2]
    tm = _pick(S, 1024, 16)
    tn = _pick(N, 1408, LANES)
    tpb = S // tm
    return pl.pallas_call(
        _in_kernel,
        out_shape=(jax.ShapeDtypeStruct((T, D), BF16), jax.ShapeDtypeStruct((T, N), BF16),
                   jax.ShapeDtypeStruct((T, LANES), F32)),
        grid=(T // tm, N // tn),
        in_specs=[pl.BlockSpec((tm, D), lambda i, j: (i, 0)),
                  pl.BlockSpec((1, D), lambda i, j: (0, 0)),
                  pl.BlockSpec((1, 1, D), lambda i, j: (i // tpb, 0, 0)),
                  pl.BlockSpec((1, 1, D), lambda i, j: (i // tpb, 0, 0)),
                  pl.BlockSpec((None, D, tn), lambda i, j: (l, 0, j)),
                  pl.BlockSpec((None, D, LANES), lambda i, j: (l, 0, 0))],
        out_specs=(pl.BlockSpec((tm, D), lambda i, j: (i, 0)),
                   pl.BlockSpec((tm, tn), lambda i, j: (i, j)),
                   pl.BlockSpec((tm, LANES), lambda i, j: (i, 0))),
        compiler_params=_params(("arbitrary", "arbitrary")),
        name="in_proj",
    )(x2, g, sh, sc, w_main, w_f)


def _pool_kernel(z_ref, w_ref, s_ref, o_ref):
    S = z_ref.shape[1]
    G, GD, _ = w_ref.shape
    row = lax.broadcasted_iota(I32, (S, GD), 0)
    for g in range(G):
        wnd = POOL_WINDOWS[g]
        a = z_ref[0, :, g * GD:(g + 1) * GD].astype(F32)
        s = a
        k = 1
        while k < wnd:
            s = s + jnp.where(row >= k, pltpu.roll(s, k, axis=0), 0.0)
            k *= 2
        cnt = jnp.minimum(row + 1, wnd).astype(F32)
        mixed = s / cnt - a
        y = jnp.dot(mixed.astype(BF16), w_ref[g], preferred_element_type=F32)
        o_ref[0, :, g * GD:(g + 1) * GD] = (y * s_ref[:, g * GD:(g + 1) * GD]).astype(BF16)


def _pool(z3, pool_w, pool_scale):
    B, S, _ = z3.shape
    G, GD, _ = pool_w.shape
    P = G * GD
    return pl.pallas_call(
        _pool_kernel,
        out_shape=jax.ShapeDtypeStruct((B, S, P), BF16),
        grid=(B,),
        in_specs=[pl.BlockSpec((1, S, P), lambda b: (b, 0, 0)),
                  pl.BlockSpec((G, GD, GD), lambda b: (0, 0, 0)),
                  pl.BlockSpec((1, P), lambda b: (0, 0))],
        out_specs=pl.BlockSpec((1, S, P), lambda b: (b, 0, 0)),
        compiler_params=_params(("arbitrary",)),
        name="pool_mixer",
    )(z3, pool_w, pool_scale)


CONV_ROWS = 64


def _conv_kernel(za_ref, zb_ref, ha_ref, hb_ref, w_ref, b_ref, g_ref, beta_ref, o_ref, ext_ref, sh_ref):
    ts = za_ref.shape[1]
    C = za_ref.shape[2]
    halo = ha_ref[0].astype(F32) * _sigmoid(hb_ref[0].astype(F32))
    ext_ref[0:CONV_HALO, :] = jnp.where(pl.program_id(1) > 0, halo, 0.0)
    ext_ref[CONV_HALO:CONV_HALO + ts, :] = za_ref[0].astype(F32) * _sigmoid(zb_ref[0].astype(F32))
    n_sh = CONV_HALO + ts - SUBLANES
    sh_ref[0] = ext_ref[...]
    for b in range(1, SUBLANES):
        sh_ref[b, 0:n_sh, :] = ext_ref[b:b + n_sh, :]
    first = CONV_HALO - (CONV_KERNEL - 1)
    R = min(CONV_ROWS, ts)
    for r0 in range(0, ts, R):
        acc = jnp.broadcast_to(b_ref[...], (R, C))
        for j in range(CONV_KERNEL):
            a, b = divmod(first + j, SUBLANES)
            acc = acc + w_ref[j:j + 1, :] * sh_ref[b, r0 + a * SUBLANES:r0 + a * SUBLANES + R, :]
        mu = jnp.mean(acc, axis=-1, keepdims=True)
        xc = acc - mu
        var = jnp.mean(xc * xc, axis=-1, keepdims=True)
        y = xc * lax.rsqrt(var + EPS) * g_ref[...] + beta_ref[...]
        o_ref[0, r0:r0 + R, :] = (y * _sigmoid(y)).astype(BF16)


def _conv(z3, off_a, conv_w, conv_b, ln_g, ln_b):
    B, S, _ = z3.shape
    K, C = conv_w.shape
    ts = _pick(S, 512, CONV_HALO)
    ia, ib = off_a // C, off_a // C + 1
    hpt = ts // CONV_HALO
    assert off_a % C == 0
    halo_map = lambda col: (lambda b, i: (b, jnp.maximum(i * hpt - 1, 0), col))
    return pl.pallas_call(
        _conv_kernel,
        out_shape=jax.ShapeDtypeStruct((B, S, C), BF16),
        grid=(B, S // ts),
        in_specs=[pl.BlockSpec((1, ts, C), lambda b, i: (b, i, ia)),
                  pl.BlockSpec((1, ts, C), lambda b, i: (b, i, ib)),
                  pl.BlockSpec((1, CONV_HALO, C), halo_map(ia)),
                  pl.BlockSpec((1, CONV_HALO, C), halo_map(ib)),
                  pl.BlockSpec((K, C), lambda b, i: (0, 0)),
                  pl.BlockSpec((1, C), lambda b, i: (0, 0)),
                  pl.BlockSpec((1, C), lambda b, i: (0, 0)),
                  pl.BlockSpec((1, C), lambda b, i: (0, 0))],
        out_specs=pl.BlockSpec((1, ts, C), lambda b, i: (b, i, 0)),
        scratch_shapes=[pltpu.VMEM((CONV_HALO + ts, C), F32),
                        pltpu.VMEM((SUBLANES, CONV_HALO + ts, C), F32)],
        compiler_params=_params(("arbitrary", "arbitrary")),
        name="conv_mixer",
    )(z3, z3, z3, z3, conv_w, conv_b, ln_g, ln_b)


CUM_CHUNK = 256


def _cumf_kernel(f_ref, b_ref, fc_ref):
    S = f_ref.shape[1]
    C = min(CUM_CHUNK, S)
    x = f_ref[0] + b_ref[...]
    ls = -(jnp.maximum(-x, 0.0) + jnp.log1p(jnp.exp(-jnp.abs(x))))
    tri = (lax.broadcasted_iota(I32, (C, C), 0) >= lax.broadcasted_iota(I32, (C, C), 1)).astype(F32)
    carry = jnp.zeros((1, LANES), F32)
    for c in range(S // C):
        cs = jnp.dot(tri, ls[c * C:(c + 1) * C], preferred_element_type=F32, precision=HIGHEST) + carry
        fc_ref[0, c * C:(c + 1) * C, :] = cs
        carry = cs[C - 1:C, :]


def _cumf(f3, fb_pad):
    B, S, _ = f3.shape
    return pl.pallas_call(
        _cumf_kernel,
        out_shape=jax.ShapeDtypeStruct((B, S, LANES), F32),
        grid=(B,),
        in_specs=[pl.BlockSpec((1, S, LANES), lambda b: (b, 0, 0)),
                  pl.BlockSpec((1, LANES), lambda b: (0, 0))],
        out_specs=pl.BlockSpec((1, S, LANES), lambda b: (b, 0, 0)),
        compiler_params=_params(("arbitrary",)),
        name="forget_cumsum",
    )(f3, fb_pad)


ATTN_STRIP = 32
LOG2E = 1.4426950408889634


def _split3(x):
    hi = x.astype(BF16).astype(F32)
    mid = (x - hi).astype(BF16).astype(F32)
    return hi, mid, x - hi - mid


def _attn_kernel(q_ref, k_ref, v_ref, fc_ref, o_ref, ka_ref, fh_ref, qa_ref, s_ref, p_ref, m_ref, l_ref, acc_ref,
                 *, scale):
    h = pl.program_id(1)
    i = pl.program_id(2)
    tq = q_ref.shape[1]
    S = k_ref.shape[1]
    nl = tq // LANES
    R = ATTN_STRIP
    q_start = pl.multiple_of(i * tq, tq)

    @pl.when(i == 0)
    def _():
        lane = lax.broadcasted_iota(I32, (S, LANES), 1)
        fh = jnp.sum(jnp.where(lane == h, fc_ref[0], 0.0), axis=1, keepdims=True) * LOG2E
        fh_ref[...] = jnp.broadcast_to(fh, (S, LANES))
        hi, mid, lo = _split3(fh)
        aug = jnp.where(lane == 0, hi, jnp.where(lane == 1, mid, jnp.where(lane == 2, lo,
                                                                           jnp.where(lane < 6, 1.0, 0.0))))
        ka_ref[:, 0:LANES] = k_ref[0]
        ka_ref[:, LANES:2 * LANES] = aug.astype(BF16)

    hi0, mid0, lo0 = _split3(fh_ref[pl.ds(q_start, SUBLANES), :][0:1, :])
    lane1 = lax.broadcasted_iota(I32, (1, LANES), 1)
    qrow = jnp.where(lane1 < 3, -1.0, jnp.where(lane1 == 3, hi0, jnp.where(lane1 == 4, mid0,
                                                                           jnp.where(lane1 == 5, lo0, 0.0))))
    qa_ref[:, 0:LANES] = (q_ref[0].astype(F32) * (scale * LOG2E)).astype(BF16)
    qa_ref[:, LANES:2 * LANES] = jnp.broadcast_to(qrow, (tq, LANES)).astype(BF16)
    m_ref[...] = jnp.full(m_ref.shape, -jnp.inf, F32)
    l_ref[...] = jnp.zeros(l_ref.shape, F32)
    acc_ref[...] = jnp.zeros(acc_ref.shape, F32)

    def logits(j, slot):
        start = pl.multiple_of(j * tq, tq)
        s_ref[slot] = lax.dot_general(qa_ref[...], ka_ref[pl.ds(start, tq), :], (((1,), (1,)), ((), ())),
                                      preferred_element_type=F32)

    def softmax_pv(j, slot, masked):
        start = pl.multiple_of(j * tq, tq)

        def scaled(r0, c):
            t = s_ref[slot, r0:r0 + R, c * LANES:(c + 1) * LANES]
            if masked:
                row = lax.broadcasted_iota(I32, (R, LANES), 0) + r0
                col = lax.broadcasted_iota(I32, (R, LANES), 1) + c * LANES
                t = jnp.where(col <= row, t, -jnp.inf)
            return t

        def live(r0):
            return [c for c in range(nl) if not masked or c * LANES < r0 + R]

        for r0 in range(0, tq, R):
            mx = scaled(r0, 0)
            for c in live(r0)[1:]:
                mx = jnp.maximum(mx, scaled(r0, c))
            m_old = m_ref[r0:r0 + R, :]
            m_new = jnp.maximum(m_old, jnp.max(mx, axis=1, keepdims=True))
            alpha = jnp.exp2(m_old - m_new)
            m_ref[r0:r0 + R, :] = m_new
            l_ref[r0:r0 + R, :] = alpha * l_ref[r0:r0 + R, :]
            acc_ref[r0:r0 + R, :] = alpha * acc_ref[r0:r0 + R, :]
        for r0 in range(0, tq, R):
            m_new = m_ref[r0:r0 + R, :]
            rs = None
            for c in range(nl):
                if c in live(r0):
                    p = jnp.exp2(scaled(r0, c) - m_new)
                    rs = p if rs is None else rs + p
                else:
                    p = jnp.zeros((R, LANES), F32)
                p_ref[r0:r0 + R, c * LANES:(c + 1) * LANES] = p.astype(BF16)
            l_ref[r0:r0 + R, :] += rs
        acc_ref[...] += jnp.dot(p_ref[...], v_ref[0, pl.ds(start, tq), :], preferred_element_type=F32)

    logits(0, 0)

    def pair(k, carry):
        logits(2 * k + 1, 1)
        softmax_pv(2 * k, 0, False)
        logits(2 * k + 2, 0)
        softmax_pv(2 * k + 1, 1, False)
        return carry

    lax.fori_loop(0, i // 2, pair, 0)

    @pl.when(i % 2 == 1)
    def _():
        logits(i, 1)
        softmax_pv(i - 1, 0, False)
        softmax_pv(i, 1, True)

    @pl.when(i % 2 == 0)
    def _():
        softmax_pv(i, 0, True)

    o_ref[0] = (acc_ref[...] / jnp.sum(l_ref[...], axis=1, keepdims=True)).astype(BF16)


def _attn(z3, off_q, H, dh, fcol):
    B, S, _ = z3.shape
    tq = _pick(S, 512, LANES)
    iq, ik, iv = off_q // dh, off_q // dh + H, off_q // dh + 2 * H
    assert off_q % dh == 0 and dh == LANES and tq % ATTN_STRIP == 0
    return pl.pallas_call(
        functools.partial(_attn_kernel, scale=1.0 / math.sqrt(dh)),
        out_shape=jax.ShapeDtypeStruct((B, S, H * dh), BF16),
        grid=(B, H, S // tq),
        in_specs=[pl.BlockSpec((1, tq, dh), lambda b, h, i: (b, i, iq + h)),
                  pl.BlockSpec((1, S, dh), lambda b, h, i: (b, 0, ik + h)),
                  pl.BlockSpec((1, S, dh), lambda b, h, i: (b, 0, iv + h)),
                  pl.BlockSpec((1, S, LANES), lambda b, h, i: (b, 0, 0))],
        out_specs=pl.BlockSpec((1, tq, dh), lambda b, h, i: (b, i, h)),
        scratch_shapes=[pltpu.VMEM((S, 2 * LANES), BF16), pltpu.VMEM((S, LANES), F32),
                        pltpu.VMEM((tq, 2 * LANES), BF16), pltpu.VMEM((2, tq, tq), F32),
                        pltpu.VMEM((tq, tq), BF16), pltpu.VMEM((tq, LANES), F32), pltpu.VMEM((tq, LANES), F32),
                        pltpu.VMEM((tq, dh), F32)],
        compiler_params=_params(("arbitrary", "arbitrary", "arbitrary")),
        name="forget_attn",
    )(z3, z3, z3, fcol)


def _gelu_tanh(x):
    return 0.5 * x * (1.0 + jnp.tanh(math.sqrt(2.0 / math.pi) * (x + 0.044715 * (x * x * x))))


def _gmlp_kernel(u_ref, v_ref, g_ref, ws_ref, bs_ref, o_ref):
    ts, GW = u_ref.shape
    G = ws_ref.shape[0]
    GD = GW // G
    ck = GMLP_CHUNK
    u = _gelu_tanh(u_ref[...].astype(F32))
    v = _gelu_tanh(v_ref[...].astype(F32))
    var = jnp.mean(v * v, axis=-1, keepdims=True)
    vb = (v * lax.rsqrt(var + EPS) * g_ref[...]).astype(BF16)
    causal = lax.broadcasted_iota(I32, (ck, ck), 0) >= lax.broadcasted_iota(I32, (ck, ck), 1)
    for g in range(G):
        wm = jnp.where(causal, ws_ref[g], 0.0).astype(BF16)
        for c in range(ts // ck):
            sv = jnp.dot(wm, vb[c * ck:(c + 1) * ck, g * GD:(g + 1) * GD], preferred_element_type=F32)
            sv = sv + bs_ref[g]
            o_ref[c * ck:(c + 1) * ck, g * GD:(g + 1) * GD] = (
                u[c * ck:(c + 1) * ck, g * GD:(g + 1) * GD] * sv).astype(BF16)


def _gmlp(z2, off_u, GW, ln_g, ws, bs_b):
    T, _ = z2.shape
    G = ws.shape[0]
    ts = _pick(T, 512, GMLP_CHUNK)
    iu = off_u // GW
    assert off_u % GW == 0
    return pl.pallas_call(
        _gmlp_kernel,
        out_shape=jax.ShapeDtypeStruct((T, GW), BF16),
        grid=(T // ts,),
        in_specs=[pl.BlockSpec((ts, GW), lambda i: (i, iu)),
                  pl.BlockSpec((ts, GW), lambda i: (i, iu + 1)),
                  pl.BlockSpec((1, GW), lambda i: (0, 0)),
                  pl.BlockSpec((G, GMLP_CHUNK, GMLP_CHUNK), lambda i: (0, 0, 0)),
                  pl.BlockSpec((G, GMLP_CHUNK, GW // G), lambda i: (0, 0, 0))],
        out_specs=pl.BlockSpec((ts, GW), lambda i: (i, 0)),
        compiler_params=_params(("arbitrary",)),
        name="gmlp_mixer",
    )(z2, z2, ln_g, ws, bs_b)


def _merge_kernel(h_ref, ya_ref, yb_ref, yc_ref, yd_ref, wg0_ref, wg1_ref, wg2_ref, wg3_ref, bg_ref,
                  wa_ref, wb_ref, wc_ref, wd_ref, o_ref):
    h = h_ref[...]
    merged = None
    for br, (y_ref, wg_ref, wbr_ref) in enumerate(((ya_ref, wg0_ref, wa_ref), (yb_ref, wg1_ref, wb_ref),
                                                   (yc_ref, wg2_ref, wc_ref), (yd_ref, wg3_ref, wd_ref))):
        gate = _sigmoid(jnp.dot(h, wg_ref[...], preferred_element_type=F32) + bg_ref[br])
        term = gate * jnp.dot(y_ref[...], wbr_ref[...], preferred_element_type=F32)
        merged = term if merged is None else merged + term
    o_ref[...] = merged.astype(BF16)


def _merge(h, ys, w_gate, b_gate4, w_branch, l, S):
    T, D = h.shape
    tm = _pick(S, 512, 16)
    tn = _pick(D, 512, LANES)
    nb = D // tn
    row = lambda width: pl.BlockSpec((tm, width), lambda i, n: (i, 0))
    wg = lambda br: pl.BlockSpec((None, D, tn), lambda i, n: (l, 0, br * nb + n))
    wb_specs, off = [], 0
    for y in ys:
        rows = y.shape[1]
        assert off % rows == 0
        wb_specs.append(pl.BlockSpec((None, rows, tn), lambda i, n, rb=off // rows: (l, rb, n)))
        off += rows
    return pl.pallas_call(
        _merge_kernel,
        out_shape=jax.ShapeDtypeStruct((T, D), BF16),
        grid=(T // tm, nb),
        in_specs=[row(D)] + [row(y.shape[1]) for y in ys] + [wg(0), wg(1), wg(2), wg(3),
                  pl.BlockSpec((4, 1, tn), lambda i, n: (0, 0, n))] + wb_specs,
        out_specs=pl.BlockSpec((tm, tn), lambda i, n: (i, n)),
        compiler_params=_params(("arbitrary", "arbitrary")),
        name="gated_merge",
    )(h, *ys, w_gate, w_gate, w_gate, w_gate, b_gate4, w_branch, w_branch, w_branch, w_branch)


def _out_kernel(m_ref, x_ref, gt_ref, wo_ref, o_ref):
    b = pl.program_id(0) // (pl.num_programs(0) // gt_ref.shape[0])
    o_ref[...] = x_ref[...] + gt_ref[b] * jnp.dot(m_ref[...], wo_ref[...], preferred_element_type=F32)


def _out_proj(merged, x2, gt, w_o, l, S):
    T, D = x2.shape
    tm = _pick(S, 512, 16)
    B = gt.shape[0]
    return pl.pallas_call(
        _out_kernel,
        out_shape=jax.ShapeDtypeStruct((T, D), F32),
        grid=(T // tm,),
        in_specs=[pl.BlockSpec((tm, D), lambda i: (i, 0)),
                  pl.BlockSpec((tm, D), lambda i: (i, 0)),
                  pl.BlockSpec((B, 1, D), lambda i: (0, 0, 0)),
                  pl.BlockSpec((None, D, D), lambda i: (l, 0, 0))],
        out_specs=pl.BlockSpec((tm, D), lambda i: (i, 0)),
        compiler_params=_params(("arbitrary",)),
        name="out_proj",
    )(merged, x2, gt, w_o)


def _pack_bf16_pairs(hb):
    W = hb.shape[1] // 2
    lo = pltpu.bitcast(hb[:, :W].astype(F32), U32)
    hi = pltpu.bitcast(hb[:, W:].astype(F32), U32)
    return (hi & jnp.uint32(0xFFFF0000)) | (lo >> 16)


def _unpack_bf16_pairs(p):
    lo = pltpu.bitcast(p << 16, F32)
    hi = pltpu.bitcast(p & jnp.uint32(0xFFFF0000), F32)
    return lo, hi


def _store_token_tiles(ref, packed):
    ns = packed.shape[1] // LANES
    for s in range(ns):
        ref[pl.ds(s, packed.shape[0], stride=ns), :] = packed[:, s * LANES:(s + 1) * LANES]


def _load_token_tiles(ref, ns):
    rows = ref.shape[0] // ns
    return jnp.concatenate([ref[pl.ds(s, rows, stride=ns), :] for s in range(ns)], axis=1)


def _token_rows(ref, t, ns):
    return ref.at[pl.ds(pl.multiple_of(t * ns, ns), ns)]


def _route_kernel(x_ref, g_ref, sh_ref, sc_ref, rw_ref, rb_ref, hp_ref, route_ref, gates_ref, cnt_ref, carry_ref):
    i = pl.program_id(0)
    tm = x_ref.shape[0]

    @pl.when(i == 0)
    def _():
        carry_ref[...] = jnp.zeros_like(carry_ref)

    hb = _norm_mod(x_ref[...], g_ref[...], sc_ref[0], sh_ref[0]).astype(BF16)
    _store_token_tiles(hp_ref, _pack_bf16_pairs(hb))
    logits = jnp.dot(hb, rw_ref[...], preferred_element_type=F32) + rb_ref[...]
    lane = lax.broadcasted_iota(I32, (tm, LANES), 1)
    lanef = lane.astype(F32)
    vals, sels = [], []
    for _ in range(TOP_K):
        m = jnp.max(logits, axis=1, keepdims=True)
        idx = jnp.min(jnp.where(logits == m, lanef, float(LANES)), axis=1, keepdims=True)
        sel = lanef == idx
        logits = jnp.where(sel, -jnp.inf, logits)
        vals.append(m)
        sels.append(sel)
    exps = [jnp.exp(v - vals[0]) for v in vals]
    denom = exps[0] + exps[1] + exps[2] + exps[3]
    onehot = jnp.zeros((tm, LANES), F32)
    for sel in sels:
        onehot = jnp.where(sel, 1.0, onehot)
    strict = (lax.broadcasted_iota(I32, (tm, tm), 0) > lax.broadcasted_iota(I32, (tm, tm), 1)).astype(BF16)
    before = jnp.dot(strict, onehot.astype(BF16), preferred_element_type=F32) + carry_ref[...]
    route = jnp.zeros((tm, LANES), F32)
    gates = jnp.zeros((tm, LANES), F32)
    for k in range(TOP_K):
        idx_k = jnp.sum(jnp.where(sels[k], lanef, 0.0), axis=1, keepdims=True)
        pos_k = jnp.sum(jnp.where(sels[k], before, 0.0), axis=1, keepdims=True)
        route = jnp.where(lane == k, idx_k, route)
        route = jnp.where(lane == TOP_K + k, pos_k, route)
        gates = jnp.where(lane == k, exps[k] / denom, gates)
    route_ref[...] = route.astype(I32)
    gates_ref[...] = gates
    carry_ref[...] += jnp.sum(onehot, axis=0, keepdims=True)
    cnt_ref[...] = carry_ref[...]


def _route(x2, g, sh, sc, rw_pad, rb_pad, S):
    T, D = x2.shape
    tm = _pick(S, 512, 16)
    tpb = S // tm
    ns = D // 2 // LANES
    return pl.pallas_call(
        _route_kernel,
        out_shape=(jax.ShapeDtypeStruct((T * ns, LANES), U32), jax.ShapeDtypeStruct((T, LANES), I32),
                   jax.ShapeDtypeStruct((T, LANES), F32), jax.ShapeDtypeStruct((1, LANES), F32)),
        grid=(T // tm,),
        in_specs=[pl.BlockSpec((tm, D), lambda i: (i, 0)),
                  pl.BlockSpec((1, D), lambda i: (0, 0)),
                  pl.BlockSpec((1, 1, D), lambda i: (i // tpb, 0, 0)),
                  pl.BlockSpec((1, 1, D), lambda i: (i // tpb, 0, 0)),
                  pl.BlockSpec((D, LANES), lambda i: (0, 0)),
                  pl.BlockSpec((1, LANES), lambda i: (0, 0))],
        out_specs=(pl.BlockSpec((tm * ns, LANES), lambda i: (i, 0)),
                   pl.BlockSpec((tm, LANES), lambda i: (i, 0)),
                   pl.BlockSpec((tm, LANES), lambda i: (i, 0)),
                   pl.BlockSpec((1, LANES), lambda i: (0, 0))),
        scratch_shapes=[pltpu.VMEM((1, LANES), F32)],
        compiler_params=_params(("arbitrary",)),
        name="moe_route",
    )(x2, g, sh, sc, rw_pad, rb_pad)


def _dispatch_kernel(dest_ref, h_ref, xs_ref, sem, *, ns):
    tm = h_ref.shape[0] // ns
    base = pl.program_id(0) * (tm * TOP_K)

    def row_copy(r, d):
        return pltpu.make_async_copy(_token_rows(h_ref, r, ns), _token_rows(xs_ref, d, ns), sem)

    def issue(r, _):
        for k in range(TOP_K):
            row_copy(r, dest_ref[base + r * TOP_K + k]).start(priority=k % 2)
        return 0

    def drain(r, _):
        for k in range(TOP_K):
            row_copy(r, dest_ref[base + r * TOP_K + k]).wait()
        return 0

    lax.fori_loop(0, tm, issue, 0)
    lax.fori_loop(0, tm, drain, 0)


def _dispatch(dest, hp, n_rows, ns):
    T = hp.shape[0] // ns
    tm = _pick(T, 512, 8)
    return pl.pallas_call(
        functools.partial(_dispatch_kernel, ns=ns),
        out_shape=jax.ShapeDtypeStruct((n_rows * ns, LANES), U32),
        grid_spec=pltpu.PrefetchScalarGridSpec(
            num_scalar_prefetch=1, grid=(T // tm,),
            in_specs=[pl.BlockSpec((tm * ns, LANES), lambda i, d: (i, 0))],
            out_specs=pl.BlockSpec(memory_space=pl.ANY),
            scratch_shapes=[pltpu.SemaphoreType.DMA(())]),
        compiler_params=_params(("arbitrary",)),
        name="moe_dispatch",
    )(dest, hp)


CAST_ROWS = 256


def _expert_kernel(be_ref, nu_ref, first_ref, nxt_ref, x_ref, wu_hbm, bu_ref, wd_hbm, bd_ref, o_ref,
                   su_ref, sd_ref, wub_ref, wdb_ref, sem, *, layer):
    b = pl.program_id(0)

    def fetch(e):
        return (pltpu.make_async_copy(wu_hbm.at[layer, e], su_ref, sem.at[0]),
                pltpu.make_async_copy(wd_hbm.at[layer, e], sd_ref, sem.at[1]))

    @pl.when(b == 0)
    def _():
        for cp in fetch(be_ref[0]):
            cp.start()

    @pl.when((b < nu_ref[0]) & (first_ref[b] == 1))
    def _():
        for cp in fetch(be_ref[b]):
            cp.wait()

        def cast(src, dst):
            chunk = _pick(src.shape[0], CAST_ROWS, 16)

            def step(r, _):
                rows = pl.ds(pl.multiple_of(r * chunk, chunk), chunk)
                dst[rows, :] = src[rows, :].astype(BF16)
                return 0
            lax.fori_loop(0, src.shape[0] // chunk, step, 0)

        cast(su_ref, wub_ref)
        cast(sd_ref, wdb_ref)

        @pl.when(nxt_ref[b] >= 0)
        def _():
            for cp in fetch(nxt_ref[b]):
                cp.start()

    @pl.when(b < nu_ref[0])
    def _():
        lo, hi = _unpack_bf16_pairs(_load_token_tiles(x_ref, wub_ref.shape[0] // (2 * LANES)))
        xb = jnp.concatenate([lo.astype(BF16), hi.astype(BF16)], axis=1)
        gu = jnp.dot(xb, wub_ref[...], preferred_element_type=F32) + bu_ref[0]
        F = gu.shape[1] // 2
        g = jnp.minimum(gu[:, :F], SWIGLU_LIMIT)
        lin = jnp.clip(gu[:, F:], -SWIGLU_LIMIT, SWIGLU_LIMIT)
        act = g * _sigmoid(SWIGLU_ALPHA * g) * (lin + 1.0)
        out = jnp.dot(act.astype(BF16), wdb_ref[...], preferred_element_type=F32) + bd_ref[0]
        _store_token_tiles(o_ref, _pack_bf16_pairs(out.astype(BF16)))


def _experts(block_e, n_used, first, nxt, xs, w_up, b_up, w_down, b_down, l):
    _, E, D, F2 = w_up.shape
    ns = D // 2 // LANES
    NP = xs.shape[0] // ns
    bm = MOE_ROWS
    blk = lambda i, be, nu, fi, nx: (jnp.minimum(i, nu[0] - 1), 0)
    return pl.pallas_call(
        functools.partial(_expert_kernel, layer=l),
        out_shape=jax.ShapeDtypeStruct((NP * ns, LANES), U32),
        grid_spec=pltpu.PrefetchScalarGridSpec(
            num_scalar_prefetch=4, grid=(NP // bm,),
            in_specs=[pl.BlockSpec((bm * ns, LANES), blk),
                      pl.BlockSpec(memory_space=pl.ANY),
                      pl.BlockSpec((1, 1, F2), lambda i, be, nu, fi, nx: (l * E + be[i], 0, 0)),
                      pl.BlockSpec(memory_space=pl.ANY),
                      pl.BlockSpec((1, 1, D), lambda i, be, nu, fi, nx: (l * E + be[i], 0, 0))],
            out_specs=pl.BlockSpec((bm * ns, LANES), blk),
            scratch_shapes=[pltpu.VMEM((D, F2), F32), pltpu.VMEM((F2 // 2, D), F32),
                            pltpu.VMEM((D, F2), BF16), pltpu.VMEM((F2 // 2, D), BF16),
                            pltpu.SemaphoreType.DMA((2,))]),
        compiler_params=_params(("arbitrary",)),
        name="moe_experts",
    )(block_e, n_used, first, nxt, xs, w_up, b_up, w_down, b_down)


def _combine_kernel(dest_ref, x_ref, gates_ref, gt_ref, ys_ref, o_ref, buf_ref, sem, g_ref=None):
    tm = x_ref.shape[0]
    W = x_ref.shape[1] // 2
    ns = W // LANES
    base = pl.program_id(0) * (tm * TOP_K)

    def row_copy(r, k, d):
        return pltpu.make_async_copy(_token_rows(ys_ref, d, ns), _token_rows(buf_ref.at[k], r, ns), sem)

    def issue(r, _):
        for k in range(TOP_K):
            row_copy(r, k, dest_ref[base + r * TOP_K + k]).start(priority=k % 2)
        return 0

    def drain(r, _):
        for k in range(TOP_K):
            row_copy(r, k, dest_ref[base + r * TOP_K + k]).wait()
        return 0

    lax.fori_loop(0, tm, issue, 0)
    lax.fori_loop(0, tm, drain, 0)
    lane = lax.broadcasted_iota(I32, (tm, LANES), 1)
    gates = gates_ref[...]
    y_lo = jnp.zeros((tm, W), F32)
    y_hi = jnp.zeros((tm, W), F32)
    for k in range(TOP_K):
        gk = jnp.sum(jnp.where(lane == k, gates, 0.0), axis=1, keepdims=True)
        lo, hi = _unpack_bf16_pairs(_load_token_tiles(buf_ref.at[k], ns))
        y_lo = y_lo + gk * lo
        y_hi = y_hi + gk * hi
    gt = gt_ref[0]
    x_lo = x_ref[:, :W] + gt[:, :W] * y_lo
    x_hi = x_ref[:, W:] + gt[:, W:] * y_hi
    if g_ref is not None:
        ssq = jnp.sum(x_lo * x_lo, axis=-1, keepdims=True) + jnp.sum(x_hi * x_hi, axis=-1, keepdims=True)
        inv = lax.rsqrt(ssq / (2 * W) + EPS)
        x_lo = x_lo * inv * g_ref[:, :W]
        x_hi = x_hi * inv * g_ref[:, W:]
    o_ref[:, :W] = x_lo
    o_ref[:, W:] = x_hi


def _combine_final_kernel(dest_ref, x_ref, gates_ref, gt_ref, ys_ref, g_ref, o_ref, buf_ref, sem):
    _combine_kernel(dest_ref, x_ref, gates_ref, gt_ref, ys_ref, o_ref, buf_ref, sem, g_ref=g_ref)


def _combine(dest, x2, gates, gt, ys, S, g_final=None):
    T, D = x2.shape
    tm = _pick(S, 256, 8)
    tpb = S // tm
    final = g_final is not None
    in_specs = [pl.BlockSpec((tm, D), lambda i, d: (i, 0)),
                pl.BlockSpec((tm, LANES), lambda i, d: (i, 0)),
                pl.BlockSpec((1, 1, D), lambda i, d: (i // tpb, 0, 0)),
                pl.BlockSpec(memory_space=pl.ANY)]
    if final:
        in_specs.append(pl.BlockSpec((1, D), lambda i, d: (0, 0)))
    return pl.pallas_call(
        _combine_final_kernel if final else _combine_kernel,
        out_shape=jax.ShapeDtypeStruct((T, D), F32),
        grid_spec=pltpu.PrefetchScalarGridSpec(
            num_scalar_prefetch=1, grid=(T // tm,),
            in_specs=in_specs,
            out_specs=pl.BlockSpec((tm, D), lambda i, d: (i, 0)),
            scratch_shapes=[pltpu.VMEM((TOP_K, tm * (D // 2 // LANES), LANES), U32),
                            pltpu.SemaphoreType.DMA(())]),
        compiler_params=_params(("arbitrary",)),
        name="moe_combine_final" if final else "moe_combine",
    )(dest, x2, gates, gt, ys, *((g_final,) if final else ()))


def _routing_tables(route, counts, n_experts, n_blocks):
    bm = MOE_ROWS
    cnt = counts[0, :n_experts].astype(I32)
    padded = (cnt + bm - 1) // bm * bm
    pend = jnp.cumsum(padded)
    pstart = pend - padded
    idx_t = route[:, :TOP_K].T
    base = jnp.zeros_like(idx_t)
    for e in range(n_experts):
        base = jnp.where(idx_t == e, pstart[e], base)
    dest = (base + route[:, TOP_K:2 * TOP_K].T).T.reshape(-1)
    first_row = jnp.arange(n_blocks, dtype=I32) * bm
    block_e = jnp.minimum(jnp.sum((pend[None, :] <= first_row[:, None]).astype(I32), axis=1), n_experts - 1)
    n_used = (pend[-1:] // bm).astype(I32)
    blk = jnp.arange(n_blocks, dtype=I32)
    first = ((blk == 0) | (block_e != jnp.roll(block_e, 1))) & (blk < n_used[0])
    start = jnp.where(first, blk, n_blocks)
    later = jnp.concatenate([lax.cummin(start[::-1])[::-1][1:], jnp.full((1,), n_blocks, I32)])
    nxt = jnp.where(later < n_blocks, block_e[jnp.minimum(later, n_blocks - 1)], -1)
    return dest, block_e, n_used, first.astype(I32), nxt.astype(I32)


def kernel(x, c, w_ada, b_ada, g_norm_mix, w_in, pool_w, pool_scale, conv_w, conv_b, conv_ln_g, conv_ln_b,
           fgate_b, gmlp_ln_g, gmlp_ws, gmlp_bs, w_gate, b_gate, w_branch, w_o, g_norm_ffn, router_w, router_b,
           w_up, b_up, w_down, b_down, g_final):
    B, S, D = x.shape
    L = w_ada.shape[0]
    T = B * S
    G, GD = pool_w.shape[1], pool_w.shape[2]
    P = G * GD
    C = conv_w.shape[2]
    H = fgate_b.shape[1]
    GW = gmlp_ln_g.shape[1]
    A = (w_in.shape[2] - P - 2 * C - H - 2 * GW) // 3
    dh = A // H
    E = router_w.shape[2]
    off_conv, off_q = P, P + 2 * C
    off_f = off_q + 3 * A
    off_gmlp = off_q + 3 * A
    n_blocks = -(-(T * TOP_K + E * (MOE_ROWS - 1)) // MOE_ROWS)

    c_pad = jnp.pad(c, ((0, (-B) % 8), (0, 0)))
    mod = _ada(c_pad, w_ada, b_ada)
    x2 = x.reshape(T, D)

    w_main = jnp.concatenate([w_in[:, :, :off_f], w_in[:, :, off_f + H:]], axis=2).astype(BF16)
    w_f = jnp.pad(w_in[:, :, off_f:off_f + H], ((0, 0), (0, 0), (0, LANES - H))).astype(BF16)
    w_gate_b = w_gate.astype(BF16)
    w_branch_b = w_branch.astype(BF16)
    w_o_b = w_o.astype(BF16)
    b_up3 = b_up.reshape(L * E, 1, -1)
    b_down3 = b_down.reshape(L * E, 1, D)

    for l in range(L):
        sh1, sc1, gt1, sh2, sc2, gt2 = (mod[l, :B, i * D:(i + 1) * D].reshape(B, 1, D) for i in range(6))
        h, z, f = _in_proj(x2, g_norm_mix[l].reshape(1, D), sh1, sc1, w_main, w_f, l, S)
        z3 = z.reshape(B, S, -1)

        y_a = _pool(z3, pool_w[l].astype(BF16), pool_scale[l].reshape(1, P))
        y_b = _conv(z3, off_conv, conv_w[l], conv_b[l].reshape(1, C), conv_ln_g[l].reshape(1, C),
                    conv_ln_b[l].reshape(1, C))
        fb_pad = jnp.pad(fgate_b[l], (0, LANES - H)).reshape(1, LANES)
        fcol = _cumf(f.reshape(B, S, LANES), fb_pad)
        y_c = _attn(z3, off_q, H, dh, fcol)
        bs_b = jnp.broadcast_to(gmlp_bs[l][:, :, None], (G, GMLP_CHUNK, GW // G))
        y_d = _gmlp(z, off_gmlp, GW, gmlp_ln_g[l].reshape(1, GW), gmlp_ws[l], bs_b)

        merged = _merge(h, (y_a.reshape(T, P), y_b.reshape(T, C), y_c.reshape(T, A), y_d), w_gate_b,
                        b_gate[l].reshape(4, 1, D), w_branch_b, l, S)
        x2 = _out_proj(merged, x2, gt1, w_o_b, l, S)

        rw_pad = jnp.pad(router_w[l], ((0, 0), (0, LANES - E))).astype(BF16)
        rb_pad = jnp.pad(router_b[l], (0, LANES - E), constant_values=-1e30).reshape(1, LANES)
        hp, route, gates, counts = _route(x2, g_norm_ffn[l].reshape(1, D), sh2, sc2, rw_pad, rb_pad, S)
        dest, block_e, n_used, first, nxt = _routing_tables(route, counts, E, n_blocks)
        xs = _dispatch(dest, hp, n_blocks * MOE_ROWS, D // 2 // LANES)
        ys = _experts(block_e, n_used, first, nxt, xs, w_up, b_up3, w_down, b_down3, l)
        x2 = _combine(dest, x2, gates, gt2, ys, S, g_final.reshape(1, D) if l == L - 1 else None)

    return x2.reshape(B, S, D)
```

```python
import functools
import math

import jax
import jax.numpy as jnp
from jax import lax
from jax.experimental import pallas as pl
from jax.experimental.pallas import tpu as pltpu

F32 = jnp.float32
BF16 = jnp.bfloat16
U32 = jnp.uint32
I32 = jnp.int32

EPS = 1e-5
POOL_WINDOWS = (2, 4, 8, 16)
CONV_KERNEL = 31
CONV_HALO = 32
GMLP_CHUNK = 128
TOP_K = 4
SWIGLU_LIMIT = 7.0
SWIGLU_ALPHA = 1.702
LANES = 128
SUBLANES = 8
MOE_ROWS = 256
VMEM_LIMIT = 56 * 1024 * 1024
HIGHEST = lax.Precision.HIGHEST


def _pick(n, target, mult):
    if n <= target:
        return n
    best = None
    for d in range(mult, target + 1, mult):
        if n % d == 0:
            best = d
    assert best is not None, (n, target, mult)
    return best


def _params(sem):
    return pltpu.CompilerParams(dimension_semantics=sem, vmem_limit_bytes=VMEM_LIMIT)


def _sigmoid(x):
    return 1.0 / (1.0 + jnp.exp(-x))


def _ada_kernel(c_ref, w_ref, b_ref, o_ref):
    c = c_ref[...]
    ca = c * _sigmoid(c)
    o_ref[0] = jnp.dot(ca, w_ref[0], preferred_element_type=F32, precision=HIGHEST) + b_ref[0]


def _ada(c_pad, w_ada, b_ada):
    L, D, N = w_ada.shape
    R = c_pad.shape[0]
    tn = _pick(N, 1024, LANES)
    return pl.pallas_call(
        _ada_kernel,
        out_shape=jax.ShapeDtypeStruct((L, R, N), F32),
        grid=(L, N // tn),
        in_specs=[pl.BlockSpec((R, D), lambda l, j: (0, 0)),
                  pl.BlockSpec((1, D, tn), lambda l, j: (l, 0, j)),
                  pl.BlockSpec((1, 1, tn), lambda l, j: (l, 0, j))],
        out_specs=pl.BlockSpec((1, R, tn), lambda l, j: (l, 0, j)),
        compiler_params=_params(("arbitrary", "arbitrary")),
        name="ada_mod",
    )(c_pad, w_ada, b_ada.reshape(L, 1, N))


def _norm_mod(x, g, sc, sh):
    var = jnp.mean(x * x, axis=-1, keepdims=True)
    y = x * lax.rsqrt(var + EPS) * g
    return y * (1.0 + sc) + sh


def _in_kernel(x_ref, g_ref, sh_ref, sc_ref, w_ref, wf_ref, h_ref, z_ref, f_ref):
    @pl.when(pl.program_id(1) == 0)
    def _():
        hb = _norm_mod(x_ref[...], g_ref[...], sc_ref[0], sh_ref[0]).astype(BF16)
        h_ref[...] = hb
        f_ref[...] = jnp.dot(hb, wf_ref[...], preferred_element_type=F32)

    z_ref[...] = jnp.dot(h_ref[...], w_ref[...], preferred_element_type=F32).astype(BF16)


def _in_proj(x2, g, sh, sc, w_main, w_f, l, S):
    T, D = x2.shape
    N = w_main.shape[2]
    tm = _pick(S, 1024, 16)
    tn = _pick(N, 1408, LANES)
    tpb = S // tm
    return pl.pallas_call(
        _in_kernel,
        out_shape=(jax.ShapeDtypeStruct((T, D), BF16), jax.ShapeDtypeStruct((T, N), BF16),
                   jax.ShapeDtypeStruct((T, LANES), F32)),
        grid=(T // tm, N // tn),
        in_specs=[pl.BlockSpec((tm, D), lambda i, j: (i, 0)),
                  pl.BlockSpec((1, D), lambda i, j: (0, 0)),
                  pl.BlockSpec((1, 1, D), lambda i, j: (i // tpb, 0, 0)),
                  pl.BlockSpec((1, 1, D), lambda i, j: (i // tpb, 0, 0)),
                  pl.BlockSpec((None, D, tn), lambda i, j: (l, 0, j)),
                  pl.BlockSpec((None, D, LANES), lambda i, j: (l, 0, 0))],
        out_specs=(pl.BlockSpec((tm, D), lambda i, j: (i, 0)),
                   pl.BlockSpec((tm, tn), lambda i, j: (i, j)),
                   pl.BlockSpec((tm, LANES), lambda i, j: (i, 0))),
        compiler_params=_params(("arbitrary", "arbitrary")),
        name="in_proj",
    )(x2, g, sh, sc, w_main, w_f)


def _pool_kernel(z_ref, w_ref, s_ref, o_ref):
    S = z_ref.shape[1]
    G, GD, _ = w_ref.shape
    row = lax.broadcasted_iota(I32, (S, GD), 0)
    for g in range(G):
        wnd = POOL_WINDOWS[g]
        a = z_ref[0, :, g * GD:(g + 1) * GD].astype(F32)
        s = a
        k = 1
        while k < wnd:
            s = s + jnp.where(row >= k, pltpu.roll(s, k, axis=0), 0.0)
            k *= 2
        cnt = jnp.minimum(row + 1, wnd).astype(F32)
        mixed = s / cnt - a
        y = jnp.dot(mixed.astype(BF16), w_ref[g], preferred_element_type=F32)
        o_ref[0, :, g * GD:(g + 1) * GD] = (y * s_ref[:, g * GD:(g + 1) * GD]).astype(BF16)


def _pool(z3, pool_w, pool_scale):
    B, S, _ = z3.shape
    G, GD, _ = pool_w.shape
    P = G * GD
    return pl.pallas_call(
        _pool_kernel,
        out_shape=jax.ShapeDtypeStruct((B, S, P), BF16),
        grid=(B,),
        in_specs=[pl.BlockSpec((1, S, P), lambda b: (b, 0, 0)),
                  pl.BlockSpec((G, GD, GD), lambda b: (0, 0, 0)),
                  pl.BlockSpec((1, P), lambda b: (0, 0))],
        out_specs=pl.BlockSpec((1, S, P), lambda b: (b, 0, 0)),
        compiler_params=_params(("arbitrary",)),
        name="pool_mixer",
    )(z3, pool_w, pool_scale)


CONV_ROWS = 64


def _conv_kernel(za_ref, zb_ref, ha_ref, hb_ref, w_ref, b_ref, g_ref, beta_ref, o_ref, ext_ref, sh_ref):
    ts = za_ref.shape[1]
    C = za_ref.shape[2]
    halo = ha_ref[0].astype(F32) * _sigmoid(hb_ref[0].astype(F32))
    ext_ref[0:CONV_HALO, :] = jnp.where(pl.program_id(1) > 0, halo, 0.0)
    ext_ref[CONV_HALO:CONV_HALO + ts, :] = za_ref[0].astype(F32) * _sigmoid(zb_ref[0].astype(F32))
    n_sh = CONV_HALO + ts - SUBLANES
    sh_ref[0] = ext_ref[...]
    for b in range(1, SUBLANES):
        sh_ref[b, 0:n_sh, :] = ext_ref[b:b + n_sh, :]
    first = CONV_HALO - (CONV_KERNEL - 1)
    R = min(CONV_ROWS, ts)
    for r0 in range(0, ts, R):
        acc = jnp.broadcast_to(b_ref[...], (R, C))
        for j in range(CONV_KERNEL):
            a, b = divmod(first + j, SUBLANES)
            acc = acc + w_ref[j:j + 1, :] * sh_ref[b, r0 + a * SUBLANES:r0 + a * SUBLANES + R, :]
        mu = jnp.mean(acc, axis=-1, keepdims=True)
        xc = acc - mu
        var = jnp.mean(xc * xc, axis=-1, keepdims=True)
        y = xc * lax.rsqrt(var + EPS) * g_ref[...] + beta_ref[...]
        o_ref[0, r0:r0 + R, :] = (y * _sigmoid(y)).astype(BF16)


def _conv(z3, off_a, conv_w, conv_b, ln_g, ln_b):
    B, S, _ = z3.shape
    K, C = conv_w.shape
    ts = _pick(S, 512, CONV_HALO)
    ia, ib = off_a // C, off_a // C + 1
    hpt = ts // CONV_HALO
    assert off_a % C == 0
    halo_map = lambda col: (lambda b, i: (b, jnp.maximum(i * hpt - 1, 0), col))
    return pl.pallas_call(
        _conv_kernel,
        out_shape=jax.ShapeDtypeStruct((B, S, C), BF16),
        grid=(B, S // ts),
        in_specs=[pl.BlockSpec((1, ts, C), lambda b, i: (b, i, ia)),
                  pl.BlockSpec((1, ts, C), lambda b, i: (b, i, ib)),
                  pl.BlockSpec((1, CONV_HALO, C), halo_map(ia)),
                  pl.BlockSpec((1, CONV_HALO, C), halo_map(ib)),
                  pl.BlockSpec((K, C), lambda b, i: (0, 0)),
                  pl.BlockSpec((1, C), lambda b, i: (0, 0)),
                  pl.BlockSpec((1, C), lambda b, i: (0, 0)),
                  pl.BlockSpec((1, C), lambda b, i: (0, 0))],
        out_specs=pl.BlockSpec((1, ts, C), lambda b, i: (b, i, 0)),
        scratch_shapes=[pltpu.VMEM((CONV_HALO + ts, C), F32),
                        pltpu.VMEM((SUBLANES, CONV_HALO + ts, C), F32)],
        compiler_params=_params(("arbitrary", "arbitrary")),
        name="conv_mixer",
    )(z3, z3, z3, z3, conv_w, conv_b, ln_g, ln_b)


CUM_CHUNK = 256


def _cumf_kernel(f_ref, b_ref, fc_ref):
    S = f_ref.shape[1]
    C = min(CUM_CHUNK, S)
    x = f_ref[0] + b_ref[...]
    ls = -(jnp.maximum(-x, 0.0) + jnp.log1p(jnp.exp(-jnp.abs(x))))
    tri = (lax.broadcasted_iota(I32, (C, C), 0) >= lax.broadcasted_iota(I32, (C, C), 1)).astype(F32)
    carry = jnp.zeros((1, LANES), F32)
    for c in range(S // C):
        cs = jnp.dot(tri, ls[c * C:(c + 1) * C], preferred_element_type=F32, precision=HIGHEST) + carry
        fc_ref[0, c * C:(c + 1) * C, :] = cs
        carry = cs[C - 1:C, :]


def _cumf(f3, fb_pad):
    B, S, _ = f3.shape
    return pl.pallas_call(
        _cumf_kernel,
        out_shape=jax.ShapeDtypeStruct((B, S, LANES), F32),
        grid=(B,),
        in_specs=[pl.BlockSpec((1, S, LANES), lambda b: (b, 0, 0)),
                  pl.BlockSpec((1, LANES), lambda b: (0, 0))],
        out_specs=pl.BlockSpec((1, S, LANES), lambda b: (b, 0, 0)),
        compiler_params=_params(("arbitrary",)),
        name="forget_cumsum",
    )(f3, fb_pad)


ATTN_STRIP = 32
LOG2E = 1.4426950408889634


def _split3(x):
    hi = x.astype(BF16).astype(F32)
    mid = (x - hi).astype(BF16).astype(F32)
    return hi, mid, x - hi - mid


def _attn_kernel(q_ref, k_ref, v_ref, fc_ref, o_ref, ka_ref, fh_ref, qa_ref, s_ref, p_ref, m_ref, l_ref, acc_ref,
                 *, scale):
    h = pl.program_id(1)
    i = pl.program_id(2)
    tq = q_ref.shape[1]
    S = k_ref.shape[1]
    nl = tq // LANES
    R = ATTN_STRIP
    q_start = pl.multiple_of(i * tq, tq)

    @pl.when(i == 0)
    def _():
        lane = lax.broadcasted_iota(I32, (S, LANES), 1)
        fh = jnp.sum(jnp.where(lane == h, fc_ref[0], 0.0), axis=1, keepdims=True) * LOG2E
        fh_ref[...] = jnp.broadcast_to(fh, (S, LANES))
        hi, mid, lo = _split3(fh)
        aug = jnp.where(lane == 0, hi, jnp.where(lane == 1, mid, jnp.where(lane == 2, lo,
                                                                           jnp.where(lane < 6, 1.0, 0.0))))
        ka_ref[:, 0:LANES] = k_ref[0]
        ka_ref[:, LANES:2 * LANES] = aug.astype(BF16)

    hi0, mid0, lo0 = _split3(fh_ref[pl.ds(q_start, SUBLANES), :][0:1, :])
    lane1 = lax.broadcasted_iota(I32, (1, LANES), 1)
    qrow = jnp.where(lane1 < 3, -1.0, jnp.where(lane1 == 3, hi0, jnp.where(lane1 == 4, mid0,
                                                                           jnp.where(lane1 == 5, lo0, 0.0))))
    qa_ref[:, 0:LANES] = (q_ref[0].astype(F32) * (scale * LOG2E)).astype(BF16)
    qa_ref[:, LANES:2 * LANES] = jnp.broadcast_to(qrow, (tq, LANES)).astype(BF16)
    m_ref[...] = jnp.full(m_ref.shape, -jnp.inf, F32)
    l_ref[...] = jnp.zeros(l_ref.shape, F32)
    acc_ref[...] = jnp.zeros(acc_ref.shape, F32)

    def logits(j, slot):
        start = pl.multiple_of(j * tq, tq)
        s_ref[slot] = lax.dot_general(qa_ref[...], ka_ref[pl.ds(start, tq), :], (((1,), (1,)), ((), ())),
                                      preferred_element_type=F32)

    def softmax_pv(j, slot, masked):
        start = pl.multiple_of(j * tq, tq)

        def scaled(r0, c):
            t = s_ref[slot, r0:r0 + R, c * LANES:(c + 1) * LANES]
            if masked:
                row = lax.broadcasted_iota(I32, (R, LANES), 0) + r0
                col = lax.broadcasted_iota(I32, (R, LANES), 1) + c * LANES
                t = jnp.where(col <= row, t, -jnp.inf)
            return t

        def live(r0):
            return [c for c in range(nl) if not masked or c * LANES < r0 + R]

        for r0 in range(0, tq, R):
            mx = scaled(r0, 0)
            for c in live(r0)[1:]:
                mx = jnp.maximum(mx, scaled(r0, c))
            m_old = m_ref[r0:r0 + R, :]
            m_new = jnp.maximum(m_old, jnp.max(mx, axis=1, keepdims=True))
            alpha = jnp.exp2(m_old - m_new)
            m_ref[r0:r0 + R, :] = m_new
            l_ref[r0:r0 + R, :] = alpha * l_ref[r0:r0 + R, :]
            acc_ref[r0:r0 + R, :] = alpha * acc_ref[r0:r0 + R, :]
        for r0 in range(0, tq, R):
            m_new = m_ref[r0:r0 + R, :]
            rs = None
            for c in range(nl):
                if c in live(r0):
                    p = jnp.exp2(scaled(r0, c) - m_new)
                    rs = p if rs is None else rs + p
                else:
                    p = jnp.zeros((R, LANES), F32)
                p_ref[r0:r0 + R, c * LANES:(c + 1) * LANES] = p.astype(BF16)
            l_ref[r0:r0 + R, :] += rs
        acc_ref[...] += jnp.dot(p_ref[...], v_ref[0, pl.ds(start, tq), :], preferred_element_type=F32)

    logits(0, 0)

    def pair(k, carry):
        logits(2 * k + 1, 1)
        softmax_pv(2 * k, 0, False)
        logits(2 * k + 2, 0)
        softmax_pv(2 * k + 1, 1, False)
        return carry

    lax.fori_loop(0, i // 2, pair, 0)

    @pl.when(i % 2 == 1)
    def _():
        logits(i, 1)
        softmax_pv(i - 1, 0, False)
        softmax_pv(i, 1, True)

    @pl.when(i % 2 == 0)
    def _():
        softmax_pv(i, 0, True)

    o_ref[0] = (acc_ref[...] / jnp.sum(l_ref[...], axis=1, keepdims=True)).astype(BF16)


def _attn(z3, off_q, H, dh, fcol):
    B, S, _ = z3.shape
    tq = _pick(S, 512, LANES)
    iq, ik, iv = off_q // dh, off_q // dh + H, off_q // dh + 2 * H
    assert off_q % dh == 0 and dh == LANES and tq % ATTN_STRIP == 0
    return pl.pallas_call(
        functools.partial(_attn_kernel, scale=1.0 / math.sqrt(dh)),
        out_shape=jax.ShapeDtypeStruct((B, S, H * dh), BF16),
        grid=(B, H, S // tq),
        in_specs=[pl.BlockSpec((1, tq, dh), lambda b, h, i: (b, i, iq + h)),
                  pl.BlockSpec((1, S, dh), lambda b, h, i: (b, 0, ik + h)),
                  pl.BlockSpec((1, S, dh), lambda b, h, i: (b, 0, iv + h)),
                  pl.BlockSpec((1, S, LANES), lambda b, h, i: (b, 0, 0))],
        out_specs=pl.BlockSpec((1, tq, dh), lambda b, h, i: (b, i, h)),
        scratch_shapes=[pltpu.VMEM((S, 2 * LANES), BF16), pltpu.VMEM((S, LANES), F32),
                        pltpu.VMEM((tq, 2 * LANES), BF16), pltpu.VMEM((2, tq, tq), F32),
                        pltpu.VMEM((tq, tq), BF16), pltpu.VMEM((tq, LANES), F32), pltpu.VMEM((tq, LANES), F32),
                        pltpu.VMEM((tq, dh), F32)],
        compiler_params=_params(("arbitrary", "arbitrary", "arbitrary")),
        name="forget_attn",
    )(z3, z3, z3, fcol)


def _gelu_tanh(x):
    return 0.5 * x * (1.0 + jnp.tanh(math.sqrt(2.0 / math.pi) * (x + 0.044715 * (x * x * x))))


def _gmlp_kernel(u_ref, v_ref, g_ref, ws_ref, bs_ref, o_ref):
    ts, GW = u_ref.shape
    G = ws_ref.shape[0]
    GD = GW // G
    ck = GMLP_CHUNK
    u = _gelu_tanh(u_ref[...].astype(F32))
    v = _gelu_tanh(v_ref[...].astype(F32))
    var = jnp.mean(v * v, axis=-1, keepdims=True)
    vb = (v * lax.rsqrt(var + EPS) * g_ref[...]).astype(BF16)
    causal = lax.broadcasted_iota(I32, (ck, ck), 0) >= lax.broadcasted_iota(I32, (ck, ck), 1)
    for g in range(G):
        wm = jnp.where(causal, ws_ref[g], 0.0).astype(BF16)
        for c in range(ts // ck):
            sv = jnp.dot(wm, vb[c * ck:(c + 1) * ck, g * GD:(g + 1) * GD], preferred_element_type=F32)
            sv = sv + bs_ref[g]
            o_ref[c * ck:(c + 1) * ck, g * GD:(g + 1) * GD] = (
                u[c * ck:(c + 1) * ck, g * GD:(g + 1) * GD] * sv).astype(BF16)


def _gmlp(z2, off_u, GW, ln_g, ws, bs_b):
    T, _ = z2.shape
    G = ws.shape[0]
    ts = _pick(T, 512, GMLP_CHUNK)
    iu = off_u // GW
    assert off_u % GW == 0
    return pl.pallas_call(
        _gmlp_kernel,
        out_shape=jax.ShapeDtypeStruct((T, GW), BF16),
        grid=(T // ts,),
        in_specs=[pl.BlockSpec((ts, GW), lambda i: (i, iu)),
                  pl.BlockSpec((ts, GW), lambda i: (i, iu + 1)),
                  pl.BlockSpec((1, GW), lambda i: (0, 0)),
                  pl.BlockSpec((G, GMLP_CHUNK, GMLP_CHUNK), lambda i: (0, 0, 0)),
                  pl.BlockSpec((G, GMLP_CHUNK, GW // G), lambda i: (0, 0, 0))],
        out_specs=pl.BlockSpec((ts, GW), lambda i: (i, 0)),
        compiler_params=_params(("arbitrary",)),
        name="gmlp_mixer",
    )(z2, z2, ln_g, ws, bs_b)


def _merge_kernel(h_ref, ya_ref, yb_ref, yc_ref, yd_ref, wg0_ref, wg1_ref, wg2_ref, wg3_ref, bg_ref,
                  wa_ref, wb_ref, wc_ref, wd_ref, o_ref):
    h = h_ref[...]
    merged = None
    for br, (y_ref, wg_ref, wbr_ref) in enumerate(((ya_ref, wg0_ref, wa_ref), (yb_ref, wg1_ref, wb_ref),
                                                   (yc_ref, wg2_ref, wc_ref), (yd_ref, wg3_ref, wd_ref))):
        gate = _sigmoid(jnp.dot(h, wg_ref[...], preferred_element_type=F32) + bg_ref[br])
        term = gate * jnp.dot(y_ref[...], wbr_ref[...], preferred_element_type=F32)
        merged = term if merged is None else merged + term
    o_ref[...] = merged.astype(BF16)


def _merge(h, ys, w_gate, b_gate4, w_branch, l, S):
    T, D = h.shape
    tm = _pick(S, 1024, 16)
    tn = _pick(D, 512, LANES)
    nb = D // tn
    row = lambda width: pl.BlockSpec((tm, width), lambda i, n: (i, 0))
    wg = lambda br: pl.BlockSpec((None, D, tn), lambda i, n: (l, 0, br * nb + n))
    wb_specs, off = [], 0
    for y in ys:
        rows = y.shape[1]
        assert off % rows == 0
        wb_specs.append(pl.BlockSpec((None, rows, tn), lambda i, n, rb=off // rows: (l, rb, n)))
        off += rows
    return pl.pallas_call(
        _merge_kernel,
        out_shape=jax.ShapeDtypeStruct((T, D), BF16),
        grid=(T // tm, nb),
        in_specs=[row(D)] + [row(y.shape[1]) for y in ys] + [wg(0), wg(1), wg(2), wg(3),
                  pl.BlockSpec((4, 1, tn), lambda i, n: (0, 0, n))] + wb_specs,
        out_specs=pl.BlockSpec((tm, tn), lambda i, n: (i, n)),
        compiler_params=_params(("arbitrary", "arbitrary")),
        name="gated_merge",
    )(h, *ys, w_gate, w_gate, w_gate, w_gate, b_gate4, w_branch, w_branch, w_branch, w_branch)


def _out_kernel(m_ref, x_ref, gt_ref, wo_ref, o_ref):
    b = pl.program_id(0) // (pl.num_programs(0) // gt_ref.shape[0])
    o_ref[...] = x_ref[...] + gt_ref[b] * jnp.dot(m_ref[...], wo_ref[...], preferred_element_type=F32)


def _out_proj(merged, x2, gt, w_o, l, S):
    T, D = x2.shape
    tm = _pick(S, 512, 16)
    B = gt.shape[0]
    return pl.pallas_call(
        _out_kernel,
        out_shape=jax.ShapeDtypeStruct((T, D), F32),
        grid=(T // tm,),
        in_specs=[pl.BlockSpec((tm, D), lambda i: (i, 0)),
                  pl.BlockSpec((tm, D), lambda i: (i, 0)),
                  pl.BlockSpec((B, 1, D), lambda i: (0, 0, 0)),
                  pl.BlockSpec((None, D, D), lambda i: (l, 0, 0))],
        out_specs=pl.BlockSpec((tm, D), lambda i: (i, 0)),
        compiler_params=_params(("arbitrary",)),
        name="out_proj",
    )(merged, x2, gt, w_o)


def _pack_bf16_pairs(hb):
    W = hb.shape[1] // 2
    lo = pltpu.bitcast(hb[:, :W].astype(F32), U32)
    hi = pltpu.bitcast(hb[:, W:].astype(F32), U32)
    return (hi & jnp.uint32(0xFFFF0000)) | (lo >> 16)


def _unpack_bf16_pairs(p):
    lo = pltpu.bitcast(p << 16, F32)
    hi = pltpu.bitcast(p & jnp.uint32(0xFFFF0000), F32)
    return lo, hi


def _store_token_tiles(ref, packed):
    ns = packed.shape[1] // LANES
    for s in range(ns):
        ref[pl.ds(s, packed.shape[0], stride=ns), :] = packed[:, s * LANES:(s + 1) * LANES]


def _load_token_tiles(ref, ns):
    rows = ref.shape[0] // ns
    return jnp.concatenate([ref[pl.ds(s, rows, stride=ns), :] for s in range(ns)], axis=1)


def _token_rows(ref, t, ns):
    return ref.at[pl.ds(pl.multiple_of(t * ns, ns), ns)]


def _route_kernel(x_ref, g_ref, sh_ref, sc_ref, rw_ref, rb_ref, hp_ref, route_ref, gates_ref, cnt_ref, carry_ref):
    i = pl.program_id(0)
    tm = x_ref.shape[0]

    @pl.when(i == 0)
    def _():
        carry_ref[...] = jnp.zeros_like(carry_ref)

    hb = _norm_mod(x_ref[...], g_ref[...], sc_ref[0], sh_ref[0]).astype(BF16)
    _store_token_tiles(hp_ref, _pack_bf16_pairs(hb))
    logits = jnp.dot(hb, rw_ref[...], preferred_element_type=F32) + rb_ref[...]
    lane = lax.broadcasted_iota(I32, (tm, LANES), 1)
    lanef = lane.astype(F32)
    vals, sels = [], []
    for _ in range(TOP_K):
        m = jnp.max(logits, axis=1, keepdims=True)
        idx = jnp.min(jnp.where(logits == m, lanef, float(LANES)), axis=1, keepdims=True)
        sel = lanef == idx
        logits = jnp.where(sel, -jnp.inf, logits)
        vals.append(m)
        sels.append(sel)
    exps = [jnp.exp(v - vals[0]) for v in vals]
    denom = exps[0] + exps[1] + exps[2] + exps[3]
    onehot = jnp.zeros((tm, LANES), F32)
    for sel in sels:
        onehot = jnp.where(sel, 1.0, onehot)
    strict = (lax.broadcasted_iota(I32, (tm, tm), 0) > lax.broadcasted_iota(I32, (tm, tm), 1)).astype(BF16)
    before = jnp.dot(strict, onehot.astype(BF16), preferred_element_type=F32) + carry_ref[...]
    route = jnp.zeros((tm, LANES), F32)
    gates = jnp.zeros((tm, LANES), F32)
    for k in range(TOP_K):
        idx_k = jnp.sum(jnp.where(sels[k], lanef, 0.0), axis=1, keepdims=True)
        pos_k = jnp.sum(jnp.where(sels[k], before, 0.0), axis=1, keepdims=True)
        route = jnp.where(lane == k, idx_k, route)
        route = jnp.where(lane == TOP_K + k, pos_k, route)
        gates = jnp.where(lane == k, exps[k] / denom, gates)
    route_ref[...] = route.astype(I32)
    gates_ref[...] = gates
    carry_ref[...] += jnp.sum(onehot, axis=0, keepdims=True)
    cnt_ref[...] = carry_ref[...]


def _route(x2, g, sh, sc, rw_pad, rb_pad, S):
    T, D = x2.shape
    tm = _pick(S, 512, 16)
    tpb = S // tm
    ns = D // 2 // LANES
    return pl.pallas_call(
        _route_kernel,
        out_shape=(jax.ShapeDtypeStruct((T * ns, LANES), U32), jax.ShapeDtypeStruct((T, LANES), I32),
                   jax.ShapeDtypeStruct((T, LANES), F32), jax.ShapeDtypeStruct((1, LANES), F32)),
        grid=(T // tm,),
        in_specs=[pl.BlockSpec((tm, D), lambda i: (i, 0)),
                  pl.BlockSpec((1, D), lambda i: (0, 0)),
                  pl.BlockSpec((1, 1, D), lambda i: (i // tpb, 0, 0)),
                  pl.BlockSpec((1, 1, D), lambda i: (i // tpb, 0, 0)),
                  pl.BlockSpec((D, LANES), lambda i: (0, 0)),
                  pl.BlockSpec((1, LANES), lambda i: (0, 0))],
        out_specs=(pl.BlockSpec((tm * ns, LANES), lambda i: (i, 0)),
                   pl.BlockSpec((tm, LANES), lambda i: (i, 0)),
                   pl.BlockSpec((tm, LANES), lambda i: (i, 0)),
                   pl.BlockSpec((1, LANES), lambda i: (0, 0))),
        scratch_shapes=[pltpu.VMEM((1, LANES), F32)],
        compiler_params=_params(("arbitrary",)),
        name="moe_route",
    )(x2, g, sh, sc, rw_pad, rb_pad)


def _dispatch_kernel(dest_ref, h_ref, xs_ref, sem, *, ns):
    tm = h_ref.shape[0] // ns
    base = pl.program_id(0) * (tm * TOP_K)

    def row_copy(r, d):
        return pltpu.make_async_copy(_token_rows(h_ref, r, ns), _token_rows(xs_ref, d, ns), sem)

    def issue(r, _):
        for k in range(TOP_K):
            row_copy(r, dest_ref[base + r * TOP_K + k]).start(priority=k % 2)
        return 0

    def drain(r, _):
        for k in range(TOP_K):
            row_copy(r, dest_ref[base + r * TOP_K + k]).wait()
        return 0

    lax.fori_loop(0, tm, issue, 0, unroll=2)
    lax.fori_loop(0, tm, drain, 0)


def _dispatch(dest, hp, n_rows, ns):
    T = hp.shape[0] // ns
    tm = _pick(T, 512, 8)
    return pl.pallas_call(
        functools.partial(_dispatch_kernel, ns=ns),
        out_shape=jax.ShapeDtypeStruct((n_rows * ns, LANES), U32),
        grid_spec=pltpu.PrefetchScalarGridSpec(
            num_scalar_prefetch=1, grid=(T // tm,),
            in_specs=[pl.BlockSpec((tm * ns, LANES), lambda i, d: (i, 0))],
            out_specs=pl.BlockSpec(memory_space=pl.ANY),
            scratch_shapes=[pltpu.SemaphoreType.DMA(())]),
        compiler_params=_params(("arbitrary",)),
        name="moe_dispatch",
    )(dest, hp)


CAST_ROWS = 256


def _expert_kernel(be_ref, nu_ref, first_ref, nxt_ref, x_ref, wu_hbm, bu_ref, wd_hbm, bd_ref, o_ref,
                   su_ref, sd_ref, wub_ref, wdb_ref, sem, *, layer):
    b = pl.program_id(0)

    def fetch(e):
        return (pltpu.make_async_copy(wu_hbm.at[layer, e], su_ref, sem.at[0]),
                pltpu.make_async_copy(wd_hbm.at[layer, e], sd_ref, sem.at[1]))

    @pl.when(b == 0)
    def _():
        for cp in fetch(be_ref[0]):
            cp.start()

    @pl.when((b < nu_ref[0]) & (first_ref[b] == 1))
    def _():
        for cp in fetch(be_ref[b]):
            cp.wait()

        def cast(src, dst):
            chunk = _pick(src.shape[0], CAST_ROWS, 16)

            def step(r, _):
                rows = pl.ds(pl.multiple_of(r * chunk, chunk), chunk)
                dst[rows, :] = src[rows, :].astype(BF16)
                return 0
            lax.fori_loop(0, src.shape[0] // chunk, step, 0)

        cast(su_ref, wub_ref)
        cast(sd_ref, wdb_ref)

        @pl.when(nxt_ref[b] >= 0)
        def _():
            for cp in fetch(nxt_ref[b]):
                cp.start()

    @pl.when(b < nu_ref[0])
    def _():
        lo, hi = _unpack_bf16_pairs(_load_token_tiles(x_ref, wub_ref.shape[0] // (2 * LANES)))
        xb = jnp.concatenate([lo.astype(BF16), hi.astype(BF16)], axis=1)
        gu = jnp.dot(xb, wub_ref[...], preferred_element_type=F32) + bu_ref[0]
        F = gu.shape[1] // 2
        g = jnp.minimum(gu[:, :F], SWIGLU_LIMIT)
        lin = jnp.clip(gu[:, F:], -SWIGLU_LIMIT, SWIGLU_LIMIT)
        act = g * _sigmoid(SWIGLU_ALPHA * g) * (lin + 1.0)
        out = jnp.dot(act.astype(BF16), wdb_ref[...], preferred_element_type=F32) + bd_ref[0]
        _store_token_tiles(o_ref, _pack_bf16_pairs(out.astype(BF16)))


def _experts(block_e, n_used, first, nxt, xs, w_up, b_up, w_down, b_down, l):
    _, E, D, F2 = w_up.shape
    ns = D // 2 // LANES
    NP = xs.shape[0] // ns
    bm = MOE_ROWS
    blk = lambda i, be, nu, fi, nx: (jnp.minimum(i, nu[0] - 1), 0)
    return pl.pallas_call(
        functools.partial(_expert_kernel, layer=l),
        out_shape=jax.ShapeDtypeStruct((NP * ns, LANES), U32),
        grid_spec=pltpu.PrefetchScalarGridSpec(
            num_scalar_prefetch=4, grid=(NP // bm,),
            in_specs=[pl.BlockSpec((bm * ns, LANES), blk),
                      pl.BlockSpec(memory_space=pl.ANY),
                      pl.BlockSpec((1, 1, F2), lambda i, be, nu, fi, nx: (l * E + be[i], 0, 0)),
                      pl.BlockSpec(memory_space=pl.ANY),
                      pl.BlockSpec((1, 1, D), lambda i, be, nu, fi, nx: (l * E + be[i], 0, 0))],
            out_specs=pl.BlockSpec((bm * ns, LANES), blk),
            scratch_shapes=[pltpu.VMEM((D, F2), F32), pltpu.VMEM((F2 // 2, D), F32),
                            pltpu.VMEM((D, F2), BF16), pltpu.VMEM((F2 // 2, D), BF16),
                            pltpu.SemaphoreType.DMA((2,))]),
        compiler_params=_params(("arbitrary",)),
        name="moe_experts",
    )(block_e, n_used, first, nxt, xs, w_up, b_up, w_down, b_down)


def _combine_kernel(dest_ref, x_ref, gates_ref, gt_ref, ys_ref, o_ref, buf_ref, sem, g_ref=None):
    tm = x_ref.shape[0]
    W = x_ref.shape[1] // 2
    ns = W // LANES
    base = pl.program_id(0) * (tm * TOP_K)

    def row_copy(r, k, d):
        return pltpu.make_async_copy(_token_rows(ys_ref, d, ns), _token_rows(buf_ref.at[k], r, ns), sem)

    def issue(r, _):
        for k in range(TOP_K):
            row_copy(r, k, dest_ref[base + r * TOP_K + k]).start(priority=k % 2)
        return 0

    def drain(r, _):
        for k in range(TOP_K):
            row_copy(r, k, dest_ref[base + r * TOP_K + k]).wait()
        return 0

    lax.fori_loop(0, tm, issue, 0, unroll=2)
    lax.fori_loop(0, tm, drain, 0)
    lane = lax.broadcasted_iota(I32, (tm, LANES), 1)
    gates = gates_ref[...]
    y_lo = jnp.zeros((tm, W), F32)
    y_hi = jnp.zeros((tm, W), F32)
    for k in range(TOP_K):
        gk = jnp.sum(jnp.where(lane == k, gates, 0.0), axis=1, keepdims=True)
        lo, hi = _unpack_bf16_pairs(_load_token_tiles(buf_ref.at[k], ns))
        y_lo = y_lo + gk * lo
        y_hi = y_hi + gk * hi
    gt = gt_ref[0]
    x_lo = x_ref[:, :W] + gt[:, :W] * y_lo
    x_hi = x_ref[:, W:] + gt[:, W:] * y_hi
    if g_ref is not None:
        ssq = jnp.sum(x_lo * x_lo, axis=-1, keepdims=True) + jnp.sum(x_hi * x_hi, axis=-1, keepdims=True)
        inv = lax.rsqrt(ssq / (2 * W) + EPS)
        x_lo = x_lo * inv * g_ref[:, :W]
        x_hi = x_hi * inv * g_ref[:, W:]
    o_ref[:, :W] = x_lo
    o_ref[:, W:] = x_hi


def _combine_final_kernel(dest_ref, x_ref, gates_ref, gt_ref, ys_ref, g_ref, o_ref, buf_ref, sem):
    _combine_kernel(dest_ref, x_ref, gates_ref, gt_ref, ys_ref, o_ref, buf_ref, sem, g_ref=g_ref)


def _combine(dest, x2, gates, gt, ys, S, g_final=None):
    T, D = x2.shape
    tm = _pick(S, 256, 8)
    tpb = S // tm
    final = g_final is not None
    in_specs = [pl.BlockSpec((tm, D), lambda i, d: (i, 0)),
                pl.BlockSpec((tm, LANES), lambda i, d: (i, 0)),
                pl.BlockSpec((1, 1, D), lambda i, d: (i // tpb, 0, 0)),
                pl.BlockSpec(memory_space=pl.ANY)]
    if final:
        in_specs.append(pl.BlockSpec((1, D), lambda i, d: (0, 0)))
    return pl.pallas_call(
        _combine_final_kernel if final else _combine_kernel,
        out_shape=jax.ShapeDtypeStruct((T, D), F32),
        grid_spec=pltpu.PrefetchScalarGridSpec(
            num_scalar_prefetch=1, grid=(T // tm,),
            in_specs=in_specs,
            out_specs=pl.BlockSpec((tm, D), lambda i, d: (i, 0)),
            scratch_shapes=[pltpu.VMEM((TOP_K, tm * (D // 2 // LANES), LANES), U32),
                            pltpu.SemaphoreType.DMA(())]),
        compiler_params=_params(("arbitrary",)),
        name="moe_combine_final" if final else "moe_combine",
    )(dest, x2, gates, gt, ys, *((g_final,) if final else ()))


def _routing_tables(route, counts, n_experts, n_blocks):
    bm = MOE_ROWS
    cnt = counts[0, :n_experts].astype(I32)
    padded = (cnt + bm - 1) // bm * bm
    pend = jnp.cumsum(padded)
    pstart = pend - padded
    idx_t = route[:, :TOP_K].T
    base = jnp.zeros_like(idx_t)
    for e in range(n_experts):
        base = jnp.where(idx_t == e, pstart[e], base)
    dest = (base + route[:, TOP_K:2 * TOP_K].T).T.reshape(-1)
    first_row = jnp.arange(n_blocks, dtype=I32) * bm
    block_e = jnp.minimum(jnp.sum((pend[None, :] <= first_row[:, None]).astype(I32), axis=1), n_experts - 1)
    n_used = (pend[-1:] // bm).astype(I32)
    blk = jnp.arange(n_blocks, dtype=I32)
    first = ((blk == 0) | (block_e != jnp.roll(block_e, 1))) & (blk < n_used[0])
    start = jnp.where(first, blk, n_blocks)
    later = jnp.concatenate([lax.cummin(start[::-1])[::-1][1:], jnp.full((1,), n_blocks, I32)])
    nxt = jnp.where(later < n_blocks, block_e[jnp.minimum(later, n_blocks - 1)], -1)
    return dest, block_e, n_used, first.astype(I32), nxt.astype(I32)


def kernel(x, c, w_ada, b_ada, g_norm_mix, w_in, pool_w, pool_scale, conv_w, conv_b, conv_ln_g, conv_ln_b,
           fgate_b, gmlp_ln_g, gmlp_ws, gmlp_bs, w_gate, b_gate, w_branch, w_o, g_norm_ffn, router_w, router_b,
           w_up, b_up, w_down, b_down, g_final):
    B, S, D = x.shape
    L = w_ada.shape[0]
    T = B * S
    G, GD = pool_w.shape[1], pool_w.shape[2]
    P = G * GD
    C = conv_w.shape[2]
    H = fgate_b.shape[1]
    GW = gmlp_ln_g.shape[1]
    A = (w_in.shape[2] - P - 2 * C - H - 2 * GW) // 3
    dh = A // H
    E = router_w.shape[2]
    off_conv, off_q = P, P + 2 * C
    off_f = off_q + 3 * A
    off_gmlp = off_q + 3 * A
    n_blocks = -(-(T * TOP_K + E * (MOE_ROWS - 1)) // MOE_ROWS)

    c_pad = jnp.pad(c, ((0, (-B) % 8), (0, 0)))
    mod = _ada(c_pad, w_ada, b_ada)
    x2 = x.reshape(T, D)

    w_main = jnp.concatenate([w_in[:, :, :off_f], w_in[:, :, off_f + H:]], axis=2).astype(BF16)
    w_f = jnp.pad(w_in[:, :, off_f:off_f + H], ((0, 0), (0, 0), (0, LANES - H))).astype(BF16)
    w_gate_b = w_gate.astype(BF16)
    w_branch_b = w_branch.astype(BF16)
    w_o_b = w_o.astype(BF16)
    b_up3 = b_up.reshape(L * E, 1, -1)
    b_down3 = b_down.reshape(L * E, 1, D)

    for l in range(L):
        sh1, sc1, gt1, sh2, sc2, gt2 = (mod[l, :B, i * D:(i + 1) * D].reshape(B, 1, D) for i in range(6))
        h, z, f = _in_proj(x2, g_norm_mix[l].reshape(1, D), sh1, sc1, w_main, w_f, l, S)
        z3 = z.reshape(B, S, -1)

        y_a = _pool(z3, pool_w[l].astype(BF16), pool_scale[l].reshape(1, P))
        y_b = _conv(z3, off_conv, conv_w[l], conv_b[l].reshape(1, C), conv_ln_g[l].reshape(1, C),
                    conv_ln_b[l].reshape(1, C))
        fb_pad = jnp.pad(fgate_b[l], (0, LANES - H)).reshape(1, LANES)
        fcol = _cumf(f.reshape(B, S, LANES), fb_pad)
        y_c = _attn(z3, off_q, H, dh, fcol)
        bs_b = jnp.broadcast_to(gmlp_bs[l][:, :, None], (G, GMLP_CHUNK, GW // G))
        y_d = _gmlp(z, off_gmlp, GW, gmlp_ln_g[l].reshape(1, GW), gmlp_ws[l], bs_b)

        merged = _merge(h, (y_a.reshape(T, P), y_b.reshape(T, C), y_c.reshape(T, A), y_d), w_gate_b,
                        b_gate[l].reshape(4, 1, D), w_branch_b, l, S)
        x2 = _out_proj(merged, x2, gt1, w_o_b, l, S)

        rw_pad = jnp.pad(router_w[l], ((0, 0), (0, LANES - E))).astype(BF16)
        rb_pad = jnp.pad(router_b[l], (0, LANES - E), constant_values=-1e30).reshape(1, LANES)
        hp, route, gates, counts = _route(x2, g_norm_ffn[l].reshape(1, D), sh2, sc2, rw_pad, rb_pad, S)
        dest, block_e, n_used, first, nxt = _routing_tables(route, counts, E, n_blocks)
        xs = _dispatch(dest, hp, n_blocks * MOE_ROWS, D // 2 // LANES)
        ys = _experts(block_e, n_used, first, nxt, xs, w_up, b_up3, w_down, b_down3, l)
        x2 = _combine(dest, x2, gates, gt2, ys, S, g_final.reshape(1, D) if l == L - 1 else None)

    return x2.reshape(B, S, D)
```

```python
import functools
import math

import jax
import jax.numpy as jnp
from jax import lax
from jax.experimental import pallas as pl
from jax.experimental.pallas import tpu as pltpu

F32 = jnp.float32
BF16 = jnp.bfloat16
U32 = jnp.uint32
I32 = jnp.int32

EPS = 1e-5
POOL_WINDOWS = (2, 4, 8, 16)
CONV_KERNEL = 31
CONV_HALO = 32
GMLP_CHUNK = 128
TOP_K = 4
SWIGLU_LIMIT = 7.0
SWIGLU_ALPHA = 1.702
LANES = 128
SUBLANES = 8
MOE_ROWS = 256
VMEM_LIMIT = 56 * 1024 * 1024
HIGHEST = lax.Precision.HIGHEST


def _pick(n, target, mult):
    if n <= target:
        return n
    best = None
    for d in range(mult, target + 1, mult):
        if n % d == 0:
            best = d
    assert best is not None, (n, target, mult)
    return best


def _params(sem):
    return pltpu.CompilerParams(dimension_semantics=sem, vmem_limit_bytes=VMEM_LIMIT)


def _sigmoid(x):
    return 1.0 / (1.0 + jnp.exp(-x))


def _ada_kernel(c_ref, w_ref, b_ref, o_ref):
    c = c_ref[...]
    ca = c * _sigmoid(c)
    o_ref[0] = jnp.dot(ca, w_ref[0], preferred_element_type=F32, precision=HIGHEST) + b_ref[0]


def _ada(c_pad, w_ada, b_ada):
    L, D, N = w_ada.shape
    R = c_pad.shape[0]
    tn = _pick(N, 1024, LANES)
    return pl.pallas_call(
        _ada_kernel,
        out_shape=jax.ShapeDtypeStruct((L, R, N), F32),
        grid=(L, N // tn),
        in_specs=[pl.BlockSpec((R, D), lambda l, j: (0, 0)),
                  pl.BlockSpec((1, D, tn), lambda l, j: (l, 0, j)),
                  pl.BlockSpec((1, 1, tn), lambda l, j: (l, 0, j))],
        out_specs=pl.BlockSpec((1, R, tn), lambda l, j: (l, 0, j)),
        compiler_params=_params(("arbitrary", "arbitrary")),
        name="ada_mod",
    )(c_pad, w_ada, b_ada.reshape(L, 1, N))


def _norm_mod(x, g, sc, sh):
    var = jnp.mean(x * x, axis=-1, keepdims=True)
    y = x * lax.rsqrt(var + EPS) * g
    return y * (1.0 + sc) + sh


def _in_kernel(x_ref, g_ref, sh_ref, sc_ref, w_ref, wf_ref, h_ref, z_ref, f_ref):
    @pl.when(pl.program_id(1) == 0)
    def _():
        hb = _norm_mod(x_ref[...], g_ref[...], sc_ref[0], sh_ref[0]).astype(BF16)
        h_ref[...] = hb
        f_ref[...] = jnp.dot(hb, wf_ref[...], preferred_element_type=F32)

    z_ref[...] = jnp.dot(h_ref[...], w_ref[...], preferred_element_type=F32).astype(BF16)


def _in_proj(x2, g, sh, sc, w_main, w_f, l, S):
    T, D = x2.shape
    N = w_main.shape[2]
    tm = _pick(S, 1024, 16)
    tn = _pick(N, 1408, LANES)
    tpb = S // tm
    return pl.pallas_call(
        _in_kernel,
        out_shape=(jax.ShapeDtypeStruct((T, D), BF16), jax.ShapeDtypeStruct((T, N), BF16),
                   jax.ShapeDtypeStruct((T, LANES), F32)),
        grid=(T // tm, N // tn),
        in_specs=[pl.BlockSpec((tm, D), lambda i, j: (i, 0)),
                  pl.BlockSpec((1, D), lambda i, j: (0, 0)),
                  pl.BlockSpec((1, 1, D), lambda i, j: (i // tpb, 0, 0)),
                  pl.BlockSpec((1, 1, D), lambda i, j: (i // tpb, 0, 0)),
                  pl.BlockSpec((None, D, tn), lambda i, j: (l, 0, j)),
                  pl.BlockSpec((None, D, LANES), lambda i, j: (l, 0, 0))],
        out_specs=(pl.BlockSpec((tm, D), lambda i, j: (i, 0)),
                   pl.BlockSpec((tm, tn), lambda i, j: (i, j)),
                   pl.BlockSpec((tm, LANES), lambda i, j: (i, 0))),
        compiler_params=_params(("arbitrary", "arbitrary")),
        name="in_proj",
    )(x2, g, sh, sc, w_main, w_f)


def _pool_kernel(z_ref, w_ref, s_ref, o_ref):
    S = z_ref.shape[1]
    G, GD, _ = w_ref.shape
    row = lax.broadcasted_iota(I32, (S, GD), 0)
    for g in range(G):
        wnd = POOL_WINDOWS[g]
        a = z_ref[0, :, g * GD:(g + 1) * GD].astype(F32)
        s = a
        k = 1
        while k < wnd:
            s = s + jnp.where(row >= k, pltpu.roll(s, k, axis=0), 0.0)
            k *= 2
        cnt = jnp.minimum(row + 1, wnd).astype(F32)
        mixed = s / cnt - a
        y = jnp.dot(mixed.astype(BF16), w_ref[g], preferred_element_type=F32)
        o_ref[0, :, g * GD:(g + 1) * GD] = (y * s_ref[:, g * GD:(g + 1) * GD]).astype(BF16)


def _pool(z3, pool_w, pool_scale):
    B, S, _ = z3.shape
    G, GD, _ = pool_w.shape
    P = G * GD
    return pl.pallas_call(
        _pool_kernel,
        out_shape=jax.ShapeDtypeStruct((B, S, P), BF16),
        grid=(B,),
        in_specs=[pl.BlockSpec((1, S, P), lambda b: (b, 0, 0)),
                  pl.BlockSpec((G, GD, GD), lambda b: (0, 0, 0)),
                  pl.BlockSpec((1, P), lambda b: (0, 0))],
        out_specs=pl.BlockSpec((1, S, P), lambda b: (b, 0, 0)),
        compiler_params=_params(("arbitrary",)),
        name="pool_mixer",
    )(z3, pool_w, pool_scale)


CONV_ROWS = 64


def _conv_kernel(za_ref, zb_ref, ha_ref, hb_ref, w_ref, b_ref, g_ref, beta_ref, o_ref, ext_ref, sh_ref):
    ts = za_ref.shape[1]
    C = za_ref.shape[2]
    halo = ha_ref[0].astype(F32) * _sigmoid(hb_ref[0].astype(F32))
    ext_ref[0:CONV_HALO, :] = jnp.where(pl.program_id(1) > 0, halo, 0.0)
    ext_ref[CONV_HALO:CONV_HALO + ts, :] = za_ref[0].astype(F32) * _sigmoid(zb_ref[0].astype(F32))
    n_sh = CONV_HALO + ts - SUBLANES
    sh_ref[0] = ext_ref[...]
    for b in range(1, SUBLANES):
        sh_ref[b, 0:n_sh, :] = ext_ref[b:b + n_sh, :]
    first = CONV_HALO - (CONV_KERNEL - 1)
    R = min(CONV_ROWS, ts)
    for r0 in range(0, ts, R):
        acc = jnp.broadcast_to(b_ref[...], (R, C))
        for j in range(CONV_KERNEL):
            a, b = divmod(first + j, SUBLANES)
            acc = acc + w_ref[j:j + 1, :] * sh_ref[b, r0 + a * SUBLANES:r0 + a * SUBLANES + R, :]
        mu = jnp.mean(acc, axis=-1, keepdims=True)
        xc = acc - mu
        var = jnp.mean(xc * xc, axis=-1, keepdims=True)
        y = xc * lax.rsqrt(var + EPS) * g_ref[...] + beta_ref[...]
        o_ref[0, r0:r0 + R, :] = (y * _sigmoid(y)).astype(BF16)


def _conv(z3, off_a, conv_w, conv_b, ln_g, ln_b):
    B, S, _ = z3.shape
    K, C = conv_w.shape
    ts = _pick(S, 512, CONV_HALO)
    ia, ib = off_a // C, off_a // C + 1
    hpt = ts // CONV_HALO
    assert off_a % C == 0
    halo_map = lambda col: (lambda b, i: (b, jnp.maximum(i * hpt - 1, 0), col))
    return pl.pallas_call(
        _conv_kernel,
        out_shape=jax.ShapeDtypeStruct((B, S, C), BF16),
        grid=(B, S // ts),
        in_specs=[pl.BlockSpec((1, ts, C), lambda b, i: (b, i, ia)),
                  pl.BlockSpec((1, ts, C), lambda b, i: (b, i, ib)),
                  pl.BlockSpec((1, CONV_HALO, C), halo_map(ia)),
                  pl.BlockSpec((1, CONV_HALO, C), halo_map(ib)),
                  pl.BlockSpec((K, C), lambda b, i: (0, 0)),
                  pl.BlockSpec((1, C), lambda b, i: (0, 0)),
                  pl.BlockSpec((1, C), lambda b, i: (0, 0)),
                  pl.BlockSpec((1, C), lambda b, i: (0, 0))],
        out_specs=pl.BlockSpec((1, ts, C), lambda b, i: (b, i, 0)),
        scratch_shapes=[pltpu.VMEM((CONV_HALO + ts, C), F32),
                        pltpu.VMEM((SUBLANES, CONV_HALO + ts, C), F32)],
        compiler_params=_params(("arbitrary", "arbitrary")),
        name="conv_mixer",
    )(z3, z3, z3, z3, conv_w, conv_b, ln_g, ln_b)


CUM_CHUNK = 256


def _cumf_kernel(f_ref, b_ref, fc_ref):
    S = f_ref.shape[1]
    C = min(CUM_CHUNK, S)
    x = f_ref[0] + b_ref[...]
    ls = -(jnp.maximum(-x, 0.0) + jnp.log1p(jnp.exp(-jnp.abs(x))))
    tri = (lax.broadcasted_iota(I32, (C, C), 0) >= lax.broadcasted_iota(I32, (C, C), 1)).astype(F32)
    carry = jnp.zeros((1, LANES), F32)
    for c in range(S // C):
        cs = jnp.dot(tri, ls[c * C:(c + 1) * C], preferred_element_type=F32, precision=HIGHEST) + carry
        fc_ref[0, c * C:(c + 1) * C, :] = cs
        carry = cs[C - 1:C, :]


def _cumf(f3, fb_pad):
    B, S, _ = f3.shape
    return pl.pallas_call(
        _cumf_kernel,
        out_shape=jax.ShapeDtypeStruct((B, S, LANES), F32),
        grid=(B,),
        in_specs=[pl.BlockSpec((1, S, LANES), lambda b: (b, 0, 0)),
                  pl.BlockSpec((1, LANES), lambda b: (0, 0))],
        out_specs=pl.BlockSpec((1, S, LANES), lambda b: (b, 0, 0)),
        compiler_params=_params(("arbitrary",)),
        name="forget_cumsum",
    )(f3, fb_pad)


ATTN_STRIP = 32
LOG2E = 1.4426950408889634


def _split3(x):
    hi = x.astype(BF16).astype(F32)
    mid = (x - hi).astype(BF16).astype(F32)
    return hi, mid, x - hi - mid


def _attn_kernel(q_ref, k_ref, v_ref, fc_ref, o_ref, ka_ref, fh_ref, qa_ref, s_ref, p_ref, m_ref, l_ref, acc_ref,
                 *, scale):
    h = pl.program_id(1)
    i = pl.program_id(2)
    tq = q_ref.shape[1]
    S = k_ref.shape[1]
    nl = tq // LANES
    R = ATTN_STRIP
    q_start = pl.multiple_of(i * tq, tq)

    @pl.when(i == 0)
    def _():
        lane = lax.broadcasted_iota(I32, (S, LANES), 1)
        fh = jnp.sum(jnp.where(lane == h, fc_ref[0], 0.0), axis=1, keepdims=True) * LOG2E
        fh_ref[...] = jnp.broadcast_to(fh, (S, LANES))
        hi, mid, lo = _split3(fh)
        aug = jnp.where(lane == 0, hi, jnp.where(lane == 1, mid, jnp.where(lane == 2, lo,
                                                                           jnp.where(lane < 6, 1.0, 0.0))))
        ka_ref[:, 0:LANES] = k_ref[0]
        ka_ref[:, LANES:2 * LANES] = aug.astype(BF16)

    hi0, mid0, lo0 = _split3(fh_ref[pl.ds(q_start, SUBLANES), :][0:1, :])
    lane1 = lax.broadcasted_iota(I32, (1, LANES), 1)
    qrow = jnp.where(lane1 < 3, -1.0, jnp.where(lane1 == 3, hi0, jnp.where(lane1 == 4, mid0,
                                                                           jnp.where(lane1 == 5, lo0, 0.0))))
    qa_ref[:, 0:LANES] = (q_ref[0].astype(F32) * (scale * LOG2E)).astype(BF16)
    qa_ref[:, LANES:2 * LANES] = jnp.broadcast_to(qrow, (tq, LANES)).astype(BF16)
    m_ref[...] = jnp.full(m_ref.shape, -jnp.inf, F32)
    l_ref[...] = jnp.zeros(l_ref.shape, F32)
    acc_ref[...] = jnp.zeros(acc_ref.shape, F32)

    def logits(j, slot):
        start = pl.multiple_of(j * tq, tq)
        s_ref[slot] = lax.dot_general(qa_ref[...], ka_ref[pl.ds(start, tq), :], (((1,), (1,)), ((), ())),
                                      preferred_element_type=F32)

    def softmax_pv(j, slot, masked):
        start = pl.multiple_of(j * tq, tq)

        def scaled(r0, c):
            t = s_ref[slot, r0:r0 + R, c * LANES:(c + 1) * LANES]
            if masked:
                row = lax.broadcasted_iota(I32, (R, LANES), 0) + r0
                col = lax.broadcasted_iota(I32, (R, LANES), 1) + c * LANES
                t = jnp.where(col <= row, t, -jnp.inf)
            return t

        def live(r0):
            return [c for c in range(nl) if not masked or c * LANES < r0 + R]

        for r0 in range(0, tq, R):
            mx = scaled(r0, 0)
            for c in live(r0)[1:]:
                mx = jnp.maximum(mx, scaled(r0, c))
            m_old = m_ref[r0:r0 + R, :]
            m_new = jnp.maximum(m_old, jnp.max(mx, axis=1, keepdims=True))
            alpha = jnp.exp2(m_old - m_new)
            m_ref[r0:r0 + R, :] = m_new
            l_ref[r0:r0 + R, :] = alpha * l_ref[r0:r0 + R, :]
            acc_ref[r0:r0 + R, :] = alpha * acc_ref[r0:r0 + R, :]
        for r0 in range(0, tq, R):
            m_new = m_ref[r0:r0 + R, :]
            rs = None
            for c in range(nl):
                if c in live(r0):
                    p = jnp.exp2(scaled(r0, c) - m_new)
                    rs = p if rs is None else rs + p
                else:
                    p = jnp.zeros((R, LANES), F32)
                p_ref[r0:r0 + R, c * LANES:(c + 1) * LANES] = p.astype(BF16)
            l_ref[r0:r0 + R, :] += rs
        acc_ref[...] += jnp.dot(p_ref[...], v_ref[0, pl.ds(start, tq), :], preferred_element_type=F32)

    logits(0, 0)

    def pair(k, carry):
        logits(2 * k + 1, 1)
        softmax_pv(2 * k, 0, False)
        logits(2 * k + 2, 0)
        softmax_pv(2 * k + 1, 1, False)
        return carry

    lax.fori_loop(0, i // 2, pair, 0)

    @pl.when(i % 2 == 1)
    def _():
        logits(i, 1)
        softmax_pv(i - 1, 0, False)
        softmax_pv(i, 1, True)

    @pl.when(i % 2 == 0)
    def _():
        softmax_pv(i, 0, True)

    o_ref[0] = (acc_ref[...] / jnp.sum(l_ref[...], axis=1, keepdims=True)).astype(BF16)


def _attn(z3, off_q, H, dh, fcol):
    B, S, _ = z3.shape
    tq = _pick(S, 512, LANES)
    iq, ik, iv = off_q // dh, off_q // dh + H, off_q // dh + 2 * H
    assert off_q % dh == 0 and dh == LANES and tq % ATTN_STRIP == 0
    return pl.pallas_call(
        functools.partial(_attn_kernel, scale=1.0 / math.sqrt(dh)),
        out_shape=jax.ShapeDtypeStruct((B, S, H * dh), BF16),
        grid=(B, H, S // tq),
        in_specs=[pl.BlockSpec((1, tq, dh), lambda b, h, i: (b, i, iq + h)),
                  pl.BlockSpec((1, S, dh), lambda b, h, i: (b, 0, ik + h)),
                  pl.BlockSpec((1, S, dh), lambda b, h, i: (b, 0, iv + h)),
                  pl.BlockSpec((1, S, LANES), lambda b, h, i: (b, 0, 0))],
        out_specs=pl.BlockSpec((1, tq, dh), lambda b, h, i: (b, i, h)),
        scratch_shapes=[pltpu.VMEM((S, 2 * LANES), BF16), pltpu.VMEM((S, LANES), F32),
                        pltpu.VMEM((tq, 2 * LANES), BF16), pltpu.VMEM((2, tq, tq), F32),
                        pltpu.VMEM((tq, tq), BF16), pltpu.VMEM((tq, LANES), F32), pltpu.VMEM((tq, LANES), F32),
                        pltpu.VMEM((tq, dh), F32)],
        compiler_params=_params(("arbitrary", "arbitrary", "arbitrary")),
        name="forget_attn",
    )(z3, z3, z3, fcol)


def _gelu_tanh(x):
    return 0.5 * x * (1.0 + jnp.tanh(math.sqrt(2.0 / math.pi) * (x + 0.044715 * (x * x * x))))


def _gmlp_kernel(u_ref, v_ref, g_ref, ws_ref, bs_ref, o_ref):
    ts, GW = u_ref.shape
    G = ws_ref.shape[0]
    GD = GW // G
    ck = GMLP_CHUNK
    u = _gelu_tanh(u_ref[...].astype(F32))
    v = _gelu_tanh(v_ref[...].astype(F32))
    var = jnp.mean(v * v, axis=-1, keepdims=True)
    vb = (v * lax.rsqrt(var + EPS) * g_ref[...]).astype(BF16)
    causal = lax.broadcasted_iota(I32, (ck, ck), 0) >= lax.broadcasted_iota(I32, (ck, ck), 1)
    for g in range(G):
        wm = jnp.where(causal, ws_ref[g], 0.0).astype(BF16)
        for c in range(ts // ck):
            sv = jnp.dot(wm, vb[c * ck:(c + 1) * ck, g * GD:(g + 1) * GD], preferred_element_type=F32)
            sv = sv + bs_ref[g]
            o_ref[c * ck:(c + 1) * ck, g * GD:(g + 1) * GD] = (
                u[c * ck:(c + 1) * ck, g * GD:(g + 1) * GD] * sv).astype(BF16)


def _gmlp(z2, off_u, GW, ln_g, ws, bs_b):
    T, _ = z2.shape
    G = ws.shape[0]
    ts = _pick(T, 512, GMLP_CHUNK)
    iu = off_u // GW
    assert off_u % GW == 0
    return pl.pallas_call(
        _gmlp_kernel,
        out_shape=jax.ShapeDtypeStruct((T, GW), BF16),
        grid=(T // ts,),
        in_specs=[pl.BlockSpec((ts, GW), lambda i: (i, iu)),
                  pl.BlockSpec((ts, GW), lambda i: (i, iu + 1)),
                  pl.BlockSpec((1, GW), lambda i: (0, 0)),
                  pl.BlockSpec((G, GMLP_CHUNK, GMLP_CHUNK), lambda i: (0, 0, 0)),
                  pl.BlockSpec((G, GMLP_CHUNK, GW // G), lambda i: (0, 0, 0))],
        out_specs=pl.BlockSpec((ts, GW), lambda i: (i, 0)),
        compiler_params=_params(("arbitrary",)),
        name="gmlp_mixer",
    )(z2, z2, ln_g, ws, bs_b)


def _merge_kernel(h_ref, ya_ref, yb_ref, yc_ref, yd_ref, wg0_ref, wg1_ref, wg2_ref, wg3_ref, bg_ref,
                  wa_ref, wb_ref, wc_ref, wd_ref, o_ref):
    h = h_ref[...]
    merged = None
    for br, (y_ref, wg_ref, wbr_ref) in enumerate(((ya_ref, wg0_ref, wa_ref), (yb_ref, wg1_ref, wb_ref),
                                                   (yc_ref, wg2_ref, wc_ref), (yd_ref, wg3_ref, wd_ref))):
        gate = _sigmoid(jnp.dot(h, wg_ref[...], preferred_element_type=F32) + bg_ref[br])
        term = gate * jnp.dot(y_ref[...], wbr_ref[...], preferred_element_type=F32)
        merged = term if merged is None else merged + term
    o_ref[...] = merged.astype(BF16)


def _merge(h, ys, w_gate, b_gate4, w_branch, l, S):
    T, D = h.shape
    tm = _pick(S, 1024, 16)
    tn = _pick(D, 512, LANES)
    nb = D // tn
    row = lambda width: pl.BlockSpec((tm, width), lambda i, n: (i, 0))
    wg = lambda br: pl.BlockSpec((None, D, tn), lambda i, n: (l, 0, br * nb + n))
    wb_specs, off = [], 0
    for y in ys:
        rows = y.shape[1]
        assert off % rows == 0
        wb_specs.append(pl.BlockSpec((None, rows, tn), lambda i, n, rb=off // rows: (l, rb, n)))
        off += rows
    return pl.pallas_call(
        _merge_kernel,
        out_shape=jax.ShapeDtypeStruct((T, D), BF16),
        grid=(T // tm, nb),
        in_specs=[row(D)] + [row(y.shape[1]) for y in ys] + [wg(0), wg(1), wg(2), wg(3),
                  pl.BlockSpec((4, 1, tn), lambda i, n: (0, 0, n))] + wb_specs,
        out_specs=pl.BlockSpec((tm, tn), lambda i, n: (i, n)),
        compiler_params=_params(("arbitrary", "arbitrary")),
        name="gated_merge",
    )(h, *ys, w_gate, w_gate, w_gate, w_gate, b_gate4, w_branch, w_branch, w_branch, w_branch)


def _out_kernel(m_ref, x_ref, gt_ref, wo_ref, o_ref):
    b = pl.program_id(0) // (pl.num_programs(0) // gt_ref.shape[0])
    o_ref[...] = x_ref[...] + gt_ref[b] * jnp.dot(m_ref[...], wo_ref[...], preferred_element_type=F32)


def _out_proj(merged, x2, gt, w_o, l, S):
    T, D = x2.shape
    tm = _pick(S, 512, 16)
    B = gt.shape[0]
    return pl.pallas_call(
        _out_kernel,
        out_shape=jax.ShapeDtypeStruct((T, D), F32),
        grid=(T // tm,),
        in_specs=[pl.BlockSpec((tm, D), lambda i: (i, 0)),
                  pl.BlockSpec((tm, D), lambda i: (i, 0)),
                  pl.BlockSpec((B, 1, D), lambda i: (0, 0, 0)),
                  pl.BlockSpec((None, D, D), lambda i: (l, 0, 0))],
        out_specs=pl.BlockSpec((tm, D), lambda i: (i, 0)),
        compiler_params=_params(("arbitrary",)),
        name="out_proj",
    )(merged, x2, gt, w_o)


def _pack_bf16_pairs(hb):
    W = hb.shape[1] // 2
    lo = pltpu.bitcast(hb[:, :W].astype(F32), U32)
    hi = pltpu.bitcast(hb[:, W:].astype(F32), U32)
    return (hi & jnp.uint32(0xFFFF0000)) | (lo >> 16)


def _unpack_bf16_pairs(p):
    lo = pltpu.bitcast(p << 16, F32)
    hi = pltpu.bitcast(p & jnp.uint32(0xFFFF0000), F32)
    return lo, hi


def _store_token_tiles(ref, packed):
    ns = packed.shape[1] // LANES
    for s in range(ns):
        ref[pl.ds(s, packed.shape[0], stride=ns), :] = packed[:, s * LANES:(s + 1) * LANES]


def _load_token_tiles(ref, ns):
    rows = ref.shape[0] // ns
    return jnp.concatenate([ref[pl.ds(s, rows, stride=ns), :] for s in range(ns)], axis=1)


def _token_rows(ref, t, ns):
    return ref.at[pl.ds(pl.multiple_of(t * ns, ns), ns)]


def _route_kernel(x_ref, g_ref, sh_ref, sc_ref, rw_ref, rb_ref, hp_ref, route_ref, gates_ref, cnt_ref, carry_ref):
    i = pl.program_id(0)
    tm = x_ref.shape[0]

    @pl.when(i == 0)
    def _():
        carry_ref[...] = jnp.zeros_like(carry_ref)

    hb = _norm_mod(x_ref[...], g_ref[...], sc_ref[0], sh_ref[0]).astype(BF16)
    _store_token_tiles(hp_ref, _pack_bf16_pairs(hb))
    logits = jnp.dot(hb, rw_ref[...], preferred_element_type=F32) + rb_ref[...]
    lane = lax.broadcasted_iota(I32, (tm, LANES), 1)
    lanef = lane.astype(F32)
    vals, sels = [], []
    for _ in range(TOP_K):
        m = jnp.max(logits, axis=1, keepdims=True)
        idx = jnp.min(jnp.where(logits == m, lanef, float(LANES)), axis=1, keepdims=True)
        sel = lanef == idx
        logits = jnp.where(sel, -jnp.inf, logits)
        vals.append(m)
        sels.append(sel)
    exps = [jnp.exp(v - vals[0]) for v in vals]
    denom = exps[0] + exps[1] + exps[2] + exps[3]
    onehot = jnp.zeros((tm, LANES), F32)
    for sel in sels:
        onehot = jnp.where(sel, 1.0, onehot)
    strict = (lax.broadcasted_iota(I32, (tm, tm), 0) > lax.broadcasted_iota(I32, (tm, tm), 1)).astype(BF16)
    before = jnp.dot(strict, onehot.astype(BF16), preferred_element_type=F32) + carry_ref[...]
    route = jnp.zeros((tm, LANES), F32)
    gates = jnp.zeros((tm, LANES), F32)
    for k in range(TOP_K):
        idx_k = jnp.sum(jnp.where(sels[k], lanef, 0.0), axis=1, keepdims=True)
        pos_k = jnp.sum(jnp.where(sels[k], before, 0.0), axis=1, keepdims=True)
        route = jnp.where(lane == k, idx_k, route)
        route = jnp.where(lane == TOP_K + k, pos_k, route)
        gates = jnp.where(lane == k, exps[k] / denom, gates)
    route_ref[...] = route.astype(I32)
    gates_ref[...] = gates
    carry_ref[...] += jnp.sum(onehot, axis=0, keepdims=True)
    cnt_ref[...] = carry_ref[...]


def _route(x2, g, sh, sc, rw_pad, rb_pad, S):
    T, D = x2.shape
    tm = _pick(S, 512, 16)
    tpb = S // tm
    ns = D // 2 // LANES
    return pl.pallas_call(
        _route_kernel,
        out_shape=(jax.ShapeDtypeStruct((T * ns, LANES), U32), jax.ShapeDtypeStruct((T, LANES), I32),
                   jax.ShapeDtypeStruct((T, LANES), F32), jax.ShapeDtypeStruct((1, LANES), F32)),
        grid=(T // tm,),
        in_specs=[pl.BlockSpec((tm, D), lambda i: (i, 0)),
                  pl.BlockSpec((1, D), lambda i: (0, 0)),
                  pl.BlockSpec((1, 1, D), lambda i: (i // tpb, 0, 0)),
                  pl.BlockSpec((1, 1, D), lambda i: (i // tpb, 0, 0)),
                  pl.BlockSpec((D, LANES), lambda i: (0, 0)),
                  pl.BlockSpec((1, LANES), lambda i: (0, 0))],
        out_specs=(pl.BlockSpec((tm * ns, LANES), lambda i: (i, 0)),
                   pl.BlockSpec((tm, LANES), lambda i: (i, 0)),
                   pl.BlockSpec((tm, LANES), lambda i: (i, 0)),
                   pl.BlockSpec((1, LANES), lambda i: (0, 0))),
        scratch_shapes=[pltpu.VMEM((1, LANES), F32)],
        compiler_params=_params(("arbitrary",)),
        name="moe_route",
    )(x2, g, sh, sc, rw_pad, rb_pad)


def _dispatch_kernel(dest_ref, h_ref, xs_ref, sem, *, ns):
    tm = h_ref.shape[0] // ns
    base = pl.program_id(0) * (tm * TOP_K)

    def row_copy(r, d):
        return pltpu.make_async_copy(_token_rows(h_ref, r, ns), _token_rows(xs_ref, d, ns), sem)

    def issue(r, _):
        for k in range(TOP_K):
            row_copy(r, dest_ref[base + r * TOP_K + k]).start(priority=k % 2)
        return 0

    def drain(r, _):
        for k in range(TOP_K):
            row_copy(r, dest_ref[base + r * TOP_K + k]).wait()
        return 0

    lax.fori_loop(0, tm, issue, 0, unroll=2)
    lax.fori_loop(0, tm, drain, 0)


def _dispatch(dest, hp, n_rows, ns):
    T = hp.shape[0] // ns
    tm = _pick(T, 1024, 8)
    return pl.pallas_call(
        functools.partial(_dispatch_kernel, ns=ns),
        out_shape=jax.ShapeDtypeStruct((n_rows * ns, LANES), U32),
        grid_spec=pltpu.PrefetchScalarGridSpec(
            num_scalar_prefetch=1, grid=(T // tm,),
            in_specs=[pl.BlockSpec((tm * ns, LANES), lambda i, d: (i, 0))],
            out_specs=pl.BlockSpec(memory_space=pl.ANY),
            scratch_shapes=[pltpu.SemaphoreType.DMA(())]),
        compiler_params=_params(("arbitrary",)),
        name="moe_dispatch",
    )(dest, hp)


CAST_ROWS = 256


def _expert_kernel(be_ref, nu_ref, first_ref, nxt_ref, x_ref, wu_hbm, bu_ref, wd_hbm, bd_ref, o_ref,
                   su_ref, sd_ref, wub_ref, wdb_ref, sem, *, layer):
    b = pl.program_id(0)

    def fetch(e):
        return (pltpu.make_async_copy(wu_hbm.at[layer, e], su_ref, sem.at[0]),
                pltpu.make_async_copy(wd_hbm.at[layer, e], sd_ref, sem.at[1]))

    @pl.when(b == 0)
    def _():
        for cp in fetch(be_ref[0]):
            cp.start()

    @pl.when((b < nu_ref[0]) & (first_ref[b] == 1))
    def _():
        for cp in fetch(be_ref[b]):
            cp.wait()

        def cast(src, dst):
            chunk = _pick(src.shape[0], CAST_ROWS, 16)

            def step(r, _):
                rows = pl.ds(pl.multiple_of(r * chunk, chunk), chunk)
                dst[rows, :] = src[rows, :].astype(BF16)
                return 0
            lax.fori_loop(0, src.shape[0] // chunk, step, 0)

        cast(su_ref, wub_ref)
        cast(sd_ref, wdb_ref)

        @pl.when(nxt_ref[b] >= 0)
        def _():
            for cp in fetch(nxt_ref[b]):
                cp.start()

    @pl.when(b < nu_ref[0])
    def _():
        lo, hi = _unpack_bf16_pairs(_load_token_tiles(x_ref, wub_ref.shape[0] // (2 * LANES)))
        xb = jnp.concatenate([lo.astype(BF16), hi.astype(BF16)], axis=1)
        gu = jnp.dot(xb, wub_ref[...], preferred_element_type=F32) + bu_ref[0]
        F = gu.shape[1] // 2
        g = jnp.minimum(gu[:, :F], SWIGLU_LIMIT)
        lin = jnp.clip(gu[:, F:], -SWIGLU_LIMIT, SWIGLU_LIMIT)
        act = g * _sigmoid(SWIGLU_ALPHA * g) * (lin + 1.0)
        out = jnp.dot(act.astype(BF16), wdb_ref[...], preferred_element_type=F32) + bd_ref[0]
        _store_token_tiles(o_ref, _pack_bf16_pairs(out.astype(BF16)))


def _experts(block_e, n_used, first, nxt, xs, w_up, b_up, w_down, b_down, l):
    _, E, D, F2 = w_up.shape
    ns = D // 2 // LANES
    NP = xs.shape[0] // ns
    bm = MOE_ROWS
    blk = lambda i, be, nu, fi, nx: (jnp.minimum(i, nu[0] - 1), 0)
    return pl.pallas_call(
        functools.partial(_expert_kernel, layer=l),
        out_shape=jax.ShapeDtypeStruct((NP * ns, LANES), U32),
        grid_spec=pltpu.PrefetchScalarGridSpec(
            num_scalar_prefetch=4, grid=(NP // bm,),
            in_specs=[pl.BlockSpec((bm * ns, LANES), blk),
                      pl.BlockSpec(memory_space=pl.ANY),
                      pl.BlockSpec((1, 1, F2), lambda i, be, nu, fi, nx: (l * E + be[i], 0, 0)),
                      pl.BlockSpec(memory_space=pl.ANY),
                      pl.BlockSpec((1, 1, D), lambda i, be, nu, fi, nx: (l * E + be[i], 0, 0))],
            out_specs=pl.BlockSpec((bm * ns, LANES), blk),
            scratch_shapes=[pltpu.VMEM((D, F2), F32), pltpu.VMEM((F2 // 2, D), F32),
                            pltpu.VMEM((D, F2), BF16), pltpu.VMEM((F2 // 2, D), BF16),
                            pltpu.SemaphoreType.DMA((2,))]),
        compiler_params=_params(("arbitrary",)),
        name="moe_experts",
    )(block_e, n_used, first, nxt, xs, w_up, b_up, w_down, b_down)


def _combine_kernel(dest_ref, x_ref, gates_ref, gt_ref, ys_ref, o_ref, buf_ref, sem, g_ref=None):
    tm = x_ref.shape[0]
    W = x_ref.shape[1] // 2
    ns = W // LANES
    base = pl.program_id(0) * (tm * TOP_K)

    def row_copy(r, k, d):
        return pltpu.make_async_copy(_token_rows(ys_ref, d, ns), _token_rows(buf_ref.at[k], r, ns), sem)

    def issue(r, _):
        for k in range(TOP_K):
            row_copy(r, k, dest_ref[base + r * TOP_K + k]).start(priority=k % 2)
        return 0

    def drain(r, _):
        for k in range(TOP_K):
            row_copy(r, k, dest_ref[base + r * TOP_K + k]).wait()
        return 0

    lax.fori_loop(0, tm, issue, 0, unroll=2)
    lax.fori_loop(0, tm, drain, 0)
    lane = lax.broadcasted_iota(I32, (tm, LANES), 1)
    gates = gates_ref[...]
    y_lo = jnp.zeros((tm, W), F32)
    y_hi = jnp.zeros((tm, W), F32)
    for k in range(TOP_K):
        gk = jnp.sum(jnp.where(lane == k, gates, 0.0), axis=1, keepdims=True)
        lo, hi = _unpack_bf16_pairs(_load_token_tiles(buf_ref.at[k], ns))
        y_lo = y_lo + gk * lo
        y_hi = y_hi + gk * hi
    gt = gt_ref[0]
    x_lo = x_ref[:, :W] + gt[:, :W] * y_lo
    x_hi = x_ref[:, W:] + gt[:, W:] * y_hi
    if g_ref is not None:
        ssq = jnp.sum(x_lo * x_lo, axis=-1, keepdims=True) + jnp.sum(x_hi * x_hi, axis=-1, keepdims=True)
        inv = lax.rsqrt(ssq / (2 * W) + EPS)
        x_lo = x_lo * inv * g_ref[:, :W]
        x_hi = x_hi * inv * g_ref[:, W:]
    o_ref[:, :W] = x_lo
    o_ref[:, W:] = x_hi


def _combine_final_kernel(dest_ref, x_ref, gates_ref, gt_ref, ys_ref, g_ref, o_ref, buf_ref, sem):
    _combine_kernel(dest_ref, x_ref, gates_ref, gt_ref, ys_ref, o_ref, buf_ref, sem, g_ref=g_ref)


def _combine(dest, x2, gates, gt, ys, S, g_final=None):
    T, D = x2.shape
    tm = _pick(S, 512, 8)
    tpb = S // tm
    final = g_final is not None
    in_specs = [pl.BlockSpec((tm, D), lambda i, d: (i, 0)),
                pl.BlockSpec((tm, LANES), lambda i, d: (i, 0)),
                pl.BlockSpec((1, 1, D), lambda i, d: (i // tpb, 0, 0)),
                pl.BlockSpec(memory_space=pl.ANY)]
    if final:
        in_specs.append(pl.BlockSpec((1, D), lambda i, d: (0, 0)))
    return pl.pallas_call(
        _combine_final_kernel if final else _combine_kernel,
        out_shape=jax.ShapeDtypeStruct((T, D), F32),
        grid_spec=pltpu.PrefetchScalarGridSpec(
            num_scalar_prefetch=1, grid=(T // tm,),
            in_specs=in_specs,
            out_specs=pl.BlockSpec((tm, D), lambda i, d: (i, 0)),
            scratch_shapes=[pltpu.VMEM((TOP_K, tm * (D // 2 // LANES), LANES), U32),
                            pltpu.SemaphoreType.DMA(())]),
        compiler_params=_params(("arbitrary",)),
        name="moe_combine_final" if final else "moe_combine",
    )(dest, x2, gates, gt, ys, *((g_final,) if final else ()))


def _routing_tables(route, counts, n_experts, n_blocks):
    bm = MOE_ROWS
    cnt = counts[0, :n_experts].astype(I32)
    padded = (cnt + bm - 1) // bm * bm
    pend = jnp.cumsum(padded)
    pstart = pend - padded
    idx_t = route[:, :TOP_K].T
    base = jnp.zeros_like(idx_t)
    for e in range(n_experts):
        base = jnp.where(idx_t == e, pstart[e], base)
    dest = (base + route[:, TOP_K:2 * TOP_K].T).T.reshape(-1)
    first_row = jnp.arange(n_blocks, dtype=I32) * bm
    block_e = jnp.minimum(jnp.sum((pend[None, :] <= first_row[:, None]).astype(I32), axis=1), n_experts - 1)
    n_used = (pend[-1:] // bm).astype(I32)
    blk = jnp.arange(n_blocks, dtype=I32)
    first = ((blk == 0) | (block_e != jnp.roll(block_e, 1))) & (blk < n_used[0])
    start = jnp.where(first, blk, n_blocks)
    later = jnp.concatenate([lax.cummin(start[::-1])[::-1][1:], jnp.full((1,), n_blocks, I32)])
    nxt = jnp.where(later < n_blocks, block_e[jnp.minimum(later, n_blocks - 1)], -1)
    return dest, block_e, n_used, first.astype(I32), nxt.astype(I32)


def kernel(x, c, w_ada, b_ada, g_norm_mix, w_in, pool_w, pool_scale, conv_w, conv_b, conv_ln_g, conv_ln_b,
           fgate_b, gmlp_ln_g, gmlp_ws, gmlp_bs, w_gate, b_gate, w_branch, w_o, g_norm_ffn, router_w, router_b,
           w_up, b_up, w_down, b_down, g_final):
    B, S, D = x.shape
    L = w_ada.shape[0]
    T = B * S
    G, GD = pool_w.shape[1], pool_w.shape[2]
    P = G * GD
    C = conv_w.shape[2]
    H = fgate_b.shape[1]
    GW = gmlp_ln_g.shape[1]
    A = (w_in.shape[2] - P - 2 * C - H - 2 * GW) // 3
    dh = A // H
    E = router_w.shape[2]
    off_conv, off_q = P, P + 2 * C
    off_f = off_q + 3 * A
    off_gmlp = off_q + 3 * A
    n_blocks = -(-(T * TOP_K + E * (MOE_ROWS - 1)) // MOE_ROWS)

    c_pad = jnp.pad(c, ((0, (-B) % 8), (0, 0)))
    mod = _ada(c_pad, w_ada, b_ada)
    x2 = x.reshape(T, D)

    w_main = jnp.concatenate([w_in[:, :, :off_f], w_in[:, :, off_f + H:]], axis=2).astype(BF16)
    w_f = jnp.pad(w_in[:, :, off_f:off_f + H], ((0, 0), (0, 0), (0, LANES - H))).astype(BF16)
    w_gate_b = w_gate.astype(BF16)
    w_branch_b = w_branch.astype(BF16)
    w_o_b = w_o.astype(BF16)
    b_up3 = b_up.reshape(L * E, 1, -1)
    b_down3 = b_down.reshape(L * E, 1, D)

    for l in range(L):
        sh1, sc1, gt1, sh2, sc2, gt2 = (mod[l, :B, i * D:(i + 1) * D].reshape(B, 1, D) for i in range(6))
        h, z, f = _in_proj(x2, g_norm_mix[l].reshape(1, D), sh1, sc1, w_main, w_f, l, S)
        z3 = z.reshape(B, S, -1)

        y_a = _pool(z3, pool_w[l].astype(BF16), pool_scale[l].reshape(1, P))
        y_b = _conv(z3, off_conv, conv_w[l], conv_b[l].reshape(1, C), conv_ln_g[l].reshape(1, C),
                    conv_ln_b[l].reshape(1, C))
        fb_pad = jnp.pad(fgate_b[l], (0, LANES - H)).reshape(1, LANES)
        fcol = _cumf(f.reshape(B, S, LANES), fb_pad)
        y_c = _attn(z3, off_q, H, dh, fcol)
        bs_b = jnp.broadcast_to(gmlp_bs[l][:, :, None], (G, GMLP_CHUNK, GW // G))
        y_d = _gmlp(z, off_gmlp, GW, gmlp_ln_g[l].reshape(1, GW), gmlp_ws[l], bs_b)

        merged = _merge(h, (y_a.reshape(T, P), y_b.reshape(T, C), y_c.reshape(T, A), y_d), w_gate_b,
                        b_gate[l].reshape(4, 1, D), w_branch_b, l, S)
        x2 = _out_proj(merged, x2, gt1, w_o_b, l, S)

        rw_pad = jnp.pad(router_w[l], ((0, 0), (0, LANES - E))).astype(BF16)
        rb_pad = jnp.pad(router_b[l], (0, LANES - E), constant_values=-1e30).reshape(1, LANES)
        hp, route, gates, counts = _route(x2, g_norm_ffn[l].reshape(1, D), sh2, sc2, rw_pad, rb_pad, S)
        dest, block_e, n_used, first, nxt = _routing_tables(route, counts, E, n_blocks)
        xs = _dispatch(dest, hp, n_blocks * MOE_ROWS, D // 2 // LANES)
        ys = _experts(block_e, n_used, first, nxt, xs, w_up, b_up3, w_down, b_down3, l)
        x2 = _combine(dest, x2, gates, gt2, ys, S, g_final.reshape(1, D) if l == L - 1 else None)

    return x2.reshape(B, S, D)
```
